```python
import jax, jax.numpy as jnp
from jax import lax
import numpy as np

D_MODEL = 4096
BATCH = 4
SEQ = 4096
DEPTH = 4

GRID_W = 64
CTX_LEN = 256
NORM_EPS = 1e-6
ADA_RANK = 256
N_MOD = 6
FFN_HIDDEN = 256 * (-(-8 * D_MODEL // (3 * 256)))
SSD_HEAD_DIM = 64
SSD_D_INNER = D_MODEL
SSD_HEADS = SSD_D_INNER // SSD_HEAD_DIM
SSD_GROUPS = 8
SSD_STATE = 128
SSD_CONV = 5
SSD_CHUNK = 128
SSD_CONV_DIM = SSD_D_INNER + 2 * SSD_GROUPS * SSD_STATE
MLA_HEADS = 32
MLA_Q_RANK = 1024
MLA_KV_RANK = 512
MLA_NOPE = 128
MLA_ROPE = 64
MLA_V = 128
MLA_SCALE = (MLA_NOPE + MLA_ROPE) ** -0.5
Q_BLOCK = 128
ROPE_THETA = 10000.0
NA_HEADS = 32
NA_HEAD_DIM = D_MODEL // NA_HEADS
NA_WIN_H = 8
NA_WIN_W = 16
NA_COL_BLOCK = 16
NA_BAND_W = 32
NA_SCALE = NA_HEAD_DIM ** -0.5
EVEN_SPLITS = (SSD_D_INNER,
               SSD_D_INNER + SSD_CONV_DIM,
               SSD_D_INNER + SSD_CONV_DIM + 2 * SSD_HEADS,
               SSD_D_INNER + SSD_CONV_DIM + 2 * SSD_HEADS + MLA_Q_RANK,
               SSD_D_INNER + SSD_CONV_DIM + 2 * SSD_HEADS + MLA_Q_RANK + MLA_KV_RANK)
IN_COLS = EVEN_SPLITS[-1] + MLA_ROPE
OUT_COLS = SSD_D_INNER + MLA_HEADS * MLA_V
N_EVEN = (DEPTH + 1) // 2
N_ODD = DEPTH // 2

kernel_name = 'hybrid_ssd_mla_natten_prefix_dit'


def rms_norm(x, g):
    xf = x.astype(jnp.float32)
    y = xf * lax.rsqrt(jnp.mean(xf * xf, axis=-1, keepdims=True) + NORM_EPS)
    return (y * g.astype(jnp.float32)).astype(x.dtype)


def grouped_rms_norm(x, g, n_groups):
    b, l, d = x.shape
    xg = x.reshape(b, l, n_groups, d // n_groups)
    xg = xg * lax.rsqrt(jnp.mean(xg * xg, axis=-1, keepdims=True) + NORM_EPS)
    return xg.reshape(b, l, d) * g.astype(jnp.float32)


def ada_params(cond, w1, w2, bias):
    m = (jax.nn.silu(cond) @ w1) @ w2 + bias
    return jnp.split(m[..., None, :], N_MOD, axis=-1)


def modulate(h, g, shift, scale):
    return rms_norm(h, g) * (1 + scale) + shift


def swiglu(h, w_gate, w_up, w_down):
    return (jax.nn.silu(h @ w_gate) * (h @ w_up)) @ w_down


def axial_rope_tables(n_tokens, rot_dim):
    t = jnp.arange(n_tokens, dtype=jnp.int32)
    rows = (t // GRID_W).astype(jnp.float32)
    cols = (t % GRID_W).astype(jnp.float32)
    n_freq = rot_dim // 4
    inv_freq = jnp.power(ROPE_THETA, -jnp.arange(n_freq, dtype=jnp.float32) / n_freq)
    ang = jnp.concatenate([rows[:, None] * inv_freq, cols[:, None] * inv_freq], axis=-1)
    return jnp.cos(ang), jnp.sin(ang)


def apply_axial_rope(x, cos, sin):
    b, l, h, r = x.shape
    nf = r // 4
    xr = x.reshape(b, l, h, 2, 2, nf)
    x1, x2 = xr[..., 0, :], xr[..., 1, :]
    cs = cos.reshape(l, 1, 2, nf).astype(x.dtype)
    sn = sin.reshape(l, 1, 2, nf).astype(x.dtype)
    out = jnp.stack([x1 * cs - x2 * sn, x2 * cs + x1 * sn], axis=-2)
    return out.reshape(b, l, h, r)


def dw_conv_centred(x, w, bias):
    k = w.shape[-1]
    kern = jnp.transpose(w)[:, None, :].astype(x.dtype)
    y = lax.conv_general_dilated(x, kern, window_strides=(1,), padding=[(k // 2, k // 2)],
                                 dimension_numbers=('NWC', 'WIO', 'NWC'),
                                 feature_group_count=x.shape[-1])
    return y + bias.astype(x.dtype)


def ssd_chunked_scan(xs, dt, a, bm, cm, h0):
    b, l, nh, p = xs.shape
    g, n = bm.shape[-2], bm.shape[-1]
    hg = nh // g
    t = SSD_CHUNK
    nc = l // t

    def chunks(arr):
        return jnp.moveaxis(arr.reshape((b, nc, t) + arr.shape[2:]), 1, 0)

    x_dt = chunks((xs.astype(jnp.float32) * dt[..., None]).reshape(b, l, g, hg, p))
    da = chunks((dt * a).reshape(b, l, g, hg))
    bc = chunks(bm.astype(jnp.float32))
    cc = chunks(cm.astype(jnp.float32))
    lower = jnp.tril(jnp.ones((t, t), dtype=bool))[None, :, :, None, None]

    def step(h, inp):
        xk, dak, bk, ck = inp
        acum = jnp.cumsum(dak, axis=1)
        seg = acum[:, :, None] - acum[:, None, :]
        decay = jnp.exp(jnp.where(lower, seg, -jnp.inf))
        cb = jnp.einsum('bqgn,bsgn->bqsg', ck, bk)
        y_in = jnp.einsum('bqsg,bqsgh,bsghp->bqghp', cb, decay, xk)
        y_st = jnp.einsum('bqgn,bghpn->bqghp', ck, h) * jnp.exp(acum)[..., None]
        to_end = jnp.exp(acum[:, -1:] - acum)
        h_new = (h * jnp.exp(acum[:, -1])[..., None, None]
                 + jnp.einsum('bsgn,bsgh,bsghp->bghpn', bk, to_end, xk))
        return h_new, y_in + y_st

    h_fin, ys = lax.scan(step, h0, (x_dt, da, bc, cc))
    return jnp.moveaxis(ys, 0, 1).reshape(b, l, nh, p), h_fin


def ssd_bidirectional(xsc, dtc, bmc, cmc, xsl, dtl, bml, cml, dt_bias, a_log):
    b = xsc.shape[0]
    h0 = jnp.zeros((b, SSD_GROUPS, SSD_HEADS // SSD_GROUPS, SSD_HEAD_DIM, SSD_STATE), jnp.float32)
    y_c = jnp.zeros(xsc.shape, jnp.float32)
    y_l = jnp.zeros(xsl.shape, jnp.float32)
    for direction in range(2):
        rev = (lambda u: jnp.flip(u, axis=1)) if direction == 1 else (lambda u: u)
        a = -jnp.exp(a_log[direction].astype(jnp.float32))
        bias = dt_bias[direction].astype(jnp.float32)
        dt_c = jax.nn.softplus(dtc[:, :, direction].astype(jnp.float32) + bias)
        dt_l = jax.nn.softplus(dtl[:, :, direction].astype(jnp.float32) + bias)
        y, h_ctx = ssd_chunked_scan(rev(xsc), rev(dt_c), a, rev(bmc), rev(cmc), h0)
        y_c = y_c + rev(y)
        y, _ = ssd_chunked_scan(rev(xsl), rev(dt_l), a, rev(bml), rev(cml), h_ctx)
        y_l = y_l + rev(y)
    return y_c, y_l


def ssd_finish(y, xs, z, d_skip, ssd_norm):
    b, l = z.shape[:2]
    y = (y + d_skip.astype(jnp.float32)[:, None] * xs.astype(jnp.float32)).reshape(b, l, SSD_D_INNER)
    return grouped_rms_norm(y * jax.nn.silu(z.astype(jnp.float32)), ssd_norm, SSD_GROUPS).astype(z.dtype)


def blocked_attention(q, k, v, scale):
    b, lq, h, dk = q.shape
    nb = lq // Q_BLOCK
    qb = jnp.moveaxis(q.reshape(b, nb, Q_BLOCK, h, dk), 1, 0)

    def one_block(qblk):
        s = jnp.einsum('bqhd,bkhd->bhqk', qblk, k).astype(jnp.float32) * scale
        pr = jax.nn.softmax(s, axis=-1).astype(v.dtype)
        return jnp.einsum('bhqk,bkhd->bqhd', pr, v)

    out = lax.map(one_block, qb)
    return jnp.moveaxis(out, 0, 1).reshape(b, lq, h * v.shape[-1])


def even_projections(u, w_in, conv_w, conv_b, q_norm, w_q_up, kv_norm, w_kv_up):
    b, l, _ = u.shape
    z, xbc, dt_raw, cq, ckv, k_pe = jnp.split(u @ w_in, list(EVEN_SPLITS), axis=-1)
    xbc = jax.nn.silu(dw_conv_centred(xbc, conv_w, conv_b))
    xs, bm, cm = jnp.split(xbc, [SSD_D_INNER, SSD_D_INNER + SSD_GROUPS * SSD_STATE], axis=-1)
    xs = xs.reshape(b, l, SSD_HEADS, SSD_HEAD_DIM)
    bm = bm.reshape(b, l, SSD_GROUPS, SSD_STATE)
    cm = cm.reshape(b, l, SSD_GROUPS, SSD_STATE)
    dt_raw = dt_raw.reshape(b, l, 2, SSD_HEADS)
    q = (rms_norm(cq, q_norm) @ w_q_up).reshape(b, l, MLA_HEADS, MLA_NOPE + MLA_ROPE)
    kv = (rms_norm(ckv, kv_norm) @ w_kv_up).reshape(b, l, MLA_HEADS, MLA_NOPE + MLA_V)
    k_nope, v = jnp.split(kv, [MLA_NOPE], axis=-1)
    return z, xs, bm, cm, dt_raw, q, k_nope, k_pe[:, :, None, :], v


def mla_keys(k_nope, k_pe):
    b, l, h, _ = k_nope.shape
    return jnp.concatenate([k_nope, jnp.broadcast_to(k_pe, (b, l, h, MLA_ROPE))], axis=-1)


def hybrid_ssd_mla_mixer(uc, ul, w_in, conv_w, conv_b, dt_bias, a_log, d_skip, ssd_norm,
                         q_norm, w_q_up, kv_norm, w_kv_up, w_out, rope_cos, rope_sin, ctx_out):
    zc, xsc, bmc, cmc, dtc, qc, knc, kpc, vc = even_projections(
        uc, w_in, conv_w, conv_b, q_norm, w_q_up, kv_norm, w_kv_up)
    zl, xsl, bml, cml, dtl, ql, knl, kpl, vl = even_projections(
        ul, w_in, conv_w, conv_b, q_norm, w_q_up, kv_norm, w_kv_up)
    yc, yl = ssd_bidirectional(xsc, dtc, bmc, cmc, xsl, dtl, bml, cml, dt_bias, a_log)
    ql = jnp.concatenate([ql[..., :MLA_NOPE], apply_axial_rope(ql[..., MLA_NOPE:], rope_cos, rope_sin)], axis=-1)
    kpl = apply_axial_rope(kpl, rope_cos, rope_sin)
    kc = mla_keys(knc, kpc)
    kl = mla_keys(knl, kpl)
    al = blocked_attention(ql, jnp.concatenate([kc, kl], axis=1), jnp.concatenate([vc, vl], axis=1), MLA_SCALE)
    out_l = jnp.concatenate([ssd_finish(yl, xsl, zl, d_skip, ssd_norm), al], axis=-1) @ w_out
    if not ctx_out:
        return out_l, None
    ac = blocked_attention(qc, kc, vc, MLA_SCALE)
    out_c = jnp.concatenate([ssd_finish(yc, xsc, zc, d_skip, ssd_norm), ac], axis=-1) @ w_out
    return out_l, out_c


def neighbourhood_attention_mixer(uc, ul, w_qkv, rpb, w_out, ctx_out):
    b, s, _ = ul.shape
    rows = s // GRID_W
    kh = min(NA_WIN_H, rows)
    n_cb = GRID_W // NA_COL_BLOCK
    q_l, k_l, v_l = jnp.split(ul @ w_qkv, 3, axis=-1)
    if ctx_out:
        q_c, k_c, v_c = jnp.split(uc @ w_qkv, 3, axis=-1)
    else:
        k_c, v_c = jnp.split(uc @ w_qkv[:, D_MODEL:], 2, axis=-1)
    k_c = k_c.reshape(b, CTX_LEN, NA_HEADS, NA_HEAD_DIM)
    v_c = v_c.reshape(b, CTX_LEN, NA_HEADS, NA_HEAD_DIM)
    kg = k_l.reshape(b, rows, GRID_W, NA_HEADS, NA_HEAD_DIM)
    vg = v_l.reshape(b, rows, GRID_W, NA_HEADS, NA_HEAD_DIM)
    qg = jnp.moveaxis(q_l.reshape(b, rows, n_cb, NA_COL_BLOCK, NA_HEADS, NA_HEAD_DIM), 1, 0)
    band_start = np.clip(np.arange(n_cb) * NA_COL_BLOCK - NA_WIN_W // 2, 0, GRID_W - NA_BAND_W)
    band_cols = band_start[:, None] + np.arange(NA_BAND_W)
    q_cols = np.arange(GRID_W).reshape(n_cb, NA_COL_BLOCK)
    c0 = np.clip(q_cols - NA_WIN_W // 2, 0, GRID_W - NA_WIN_W)
    col_valid = ((band_cols[:, None, :] >= c0[:, :, None])
                 & (band_cols[:, None, :] < c0[:, :, None] + NA_WIN_W))
    col_idx = np.clip(band_cols[:, None, :] - q_cols[:, :, None] + NA_WIN_W - 1, 0, 2 * NA_WIN_W - 2)
    rpb_f = rpb.astype(jnp.float32)
    n_win = kh * NA_BAND_W

    def one_row(args):
        r, q_row = args
        r0 = jnp.clip(r - kh // 2, 0, rows - kh)
        k_band = lax.dynamic_slice_in_dim(kg, r0, kh, axis=1)[:, :, band_cols]
        v_band = lax.dynamic_slice_in_dim(vg, r0, kh, axis=1)[:, :, band_cols]
        row_idx = r0 + jnp.arange(kh) - r + NA_WIN_H - 1
        bias = jnp.transpose(rpb_f[:, row_idx][:, :, col_idx], (0, 2, 3, 1, 4))
        s_win = jnp.einsum('bjqhd,brjkhd->bhjqrk', q_row, k_band).astype(jnp.float32) * NA_SCALE + bias
        s_win = jnp.where(col_valid[:, :, None, :], s_win, -jnp.inf)
        s_ctx = jnp.einsum('bjqhd,bchd->bhjqc', q_row, k_c).astype(jnp.float32) * NA_SCALE
        s_all = jnp.concatenate([s_win.reshape(b, NA_HEADS, n_cb, NA_COL_BLOCK, n_win), s_ctx], axis=-1)
        pr = jax.nn.softmax(s_all, axis=-1).astype(v_l.dtype)
        p_win = pr[..., :n_win].reshape(b, NA_HEADS, n_cb, NA_COL_BLOCK, kh, NA_BAND_W)
        out = (jnp.einsum('bhjqrk,brjkhd->bjqhd', p_win, v_band)
               + jnp.einsum('bhjqc,bchd->bjqhd', pr[..., n_win:], v_c))
        return out.reshape(b, GRID_W, NA_HEADS * NA_HEAD_DIM)

    o = lax.map(one_row, (jnp.arange(rows, dtype=jnp.int32), qg))
    out_l = jnp.moveaxis(o, 0, 1).reshape(b, s, D_MODEL) @ w_out
    if not ctx_out:
        return out_l, None
    q_c = q_c.reshape(b, CTX_LEN, NA_HEADS, NA_HEAD_DIM)
    out_c = blocked_attention(q_c, k_c, v_c, NA_SCALE) @ w_out
    return out_l, out_c


def setup_inputs(seed: int = 0) -> dict:
    key = jax.random.key(seed)
    ks = jax.random.split(key, 32)
    f32 = jnp.float32

    def nrm(k, shape, scale):
        return jax.random.normal(k, shape, f32) * scale

    def gain(k, shape):
        return 1.0 + 0.02 * jax.random.normal(k, shape, f32)

    dt0 = jnp.exp(jax.random.uniform(ks[14], (N_EVEN, 2, SSD_HEADS), f32, np.log(1e-3), np.log(1e-1)))
    return {
        'x': nrm(ks[0], (BATCH, SEQ, D_MODEL), 1.0),
        'c': nrm(ks[1], (BATCH, D_MODEL), 1.0),
        'ctx': nrm(ks[2], (BATCH, CTX_LEN, D_MODEL), 1.0),
        'c_ctx': nrm(ks[3], (D_MODEL,), 1.0),
        'ada_w1': nrm(ks[4], (DEPTH, D_MODEL, ADA_RANK), D_MODEL ** -0.5),
        'ada_w2': nrm(ks[5], (DEPTH, ADA_RANK, N_MOD * D_MODEL), 0.5 * ADA_RANK ** -0.5),
        'ada_b': nrm(ks[6], (DEPTH, N_MOD * D_MODEL), 0.01),
        'norm_mix': gain(ks[7], (DEPTH, D_MODEL)),
        'norm_ffn': gain(ks[8], (DEPTH, D_MODEL)),
        'ffn_w_gate': nrm(ks[9], (DEPTH, D_MODEL, FFN_HIDDEN), D_MODEL ** -0.5),
        'ffn_w_up': nrm(ks[10], (DEPTH, D_MODEL, FFN_HIDDEN), D_MODEL ** -0.5),
        'ffn_w_down': nrm(ks[11], (DEPTH, FFN_HIDDEN, D_MODEL), FFN_HIDDEN ** -0.5),
        'hyb_w_in': nrm(ks[12], (N_EVEN, D_MODEL, IN_COLS), D_MODEL ** -0.5),
        'ssd_conv_w': nrm(ks[13], (N_EVEN, SSD_CONV_DIM, SSD_CONV), SSD_CONV ** -0.5),
        'ssd_conv_b': nrm(ks[15], (N_EVEN, SSD_CONV_DIM), 0.01),
        'ssd_dt_bias': dt0 + jnp.log(-jnp.expm1(-dt0)),
        'ssd_a_log': jnp.log(jax.random.uniform(ks[16], (N_EVEN, 2, SSD_HEADS), f32, 1.0, 16.0)),
        'ssd_d': gain(ks[17], (N_EVEN, SSD_HEADS)),
        'ssd_norm': gain(ks[18], (N_EVEN, SSD_D_INNER)),
        'mla_q_norm': gain(ks[19], (N_EVEN, MLA_Q_RANK)),
        'mla_w_q_up': nrm(ks[20], (N_EVEN, MLA_Q_RANK, MLA_HEADS * (MLA_NOPE + MLA_ROPE)), MLA_Q_RANK ** -0.5),
        'mla_kv_norm': gain(ks[21], (N_EVEN, MLA_KV_RANK)),
        'mla_w_kv_up': nrm(ks[22], (N_EVEN, MLA_KV_RANK, MLA_HEADS * (MLA_NOPE + MLA_V)), MLA_KV_RANK ** -0.5),
        'hyb_w_out': nrm(ks[23], (N_EVEN, OUT_COLS, D_MODEL), OUT_COLS ** -0.5),
        'na_w_qkv': nrm(ks[24], (N_ODD, D_MODEL, 3 * D_MODEL), D_MODEL ** -0.5),
        'na_rpb': nrm(ks[25], (N_ODD, NA_HEADS, 2 * NA_WIN_H - 1, 2 * NA_WIN_W - 1), 0.1),
        'na_w_out': nrm(ks[26], (N_ODD, D_MODEL, D_MODEL), D_MODEL ** -0.5),
        'final_norm': gain(ks[27], (D_MODEL,)),
    }


def reference(x, c, ctx, c_ctx, ada_w1, ada_w2, ada_b, norm_mix, norm_ffn, ffn_w_gate, ffn_w_up, ffn_w_down,
              hyb_w_in, ssd_conv_w, ssd_conv_b, ssd_dt_bias, ssd_a_log, ssd_d, ssd_norm,
              mla_q_norm, mla_w_q_up, mla_kv_norm, mla_w_kv_up, hyb_w_out,
              na_w_qkv, na_rpb, na_w_out, final_norm):
    rope_cos, rope_sin = axial_rope_tables(x.shape[1], MLA_ROPE)
    hl = x
    hc = ctx
    for layer in range(DEPTH):
        ctx_out = layer < DEPTH - 1
        sh1, sc1, g1, sh2, sc2, g2 = ada_params(c, ada_w1[layer], ada_w2[layer], ada_b[layer])
        csh1, csc1, cg1, csh2, csc2, cg2 = ada_params(c_ctx, ada_w1[layer], ada_w2[layer], ada_b[layer])
        ul = modulate(hl, norm_mix[layer], sh1, sc1)
        uc = modulate(hc, norm_mix[layer], csh1, csc1)
        if layer % 2 == 0:
            e = layer // 2
            ml, mc = hybrid_ssd_mla_mixer(uc, ul, hyb_w_in[e], ssd_conv_w[e], ssd_conv_b[e], ssd_dt_bias[e],
                                          ssd_a_log[e], ssd_d[e], ssd_norm[e], mla_q_norm[e], mla_w_q_up[e],
                                          mla_kv_norm[e], mla_w_kv_up[e], hyb_w_out[e], rope_cos, rope_sin,
                                          ctx_out)
        else:
            o = layer // 2
            ml, mc = neighbourhood_attention_mixer(uc, ul, na_w_qkv[o], na_rpb[o], na_w_out[o], ctx_out)
        hl = hl + g1 * ml
        hl = hl + g2 * swiglu(modulate(hl, norm_ffn[layer], sh2, sc2),
                              ffn_w_gate[layer], ffn_w_up[layer], ffn_w_down[layer])
        if ctx_out:
            hc = hc + cg1 * mc
            hc = hc + cg2 * swiglu(modulate(hc, norm_ffn[layer], csh2, csc2),
                                   ffn_w_gate[layer], ffn_w_up[layer], ffn_w_down[layer])
    return rms_norm(hl, final_norm)
```

```python
import functools

import numpy as np
import jax
import jax.numpy as jnp
from jax import lax
from jax.experimental import pallas as pl
from jax.experimental.pallas import tpu as pltpu

F32 = jnp.float32
BF16 = jnp.bfloat16

NORM_EPS = 1e-6
N_MOD = 6
GRID_W = 64
ROPE_THETA = 10000.0
SSD_HEAD_DIM = 64
SSD_GROUPS = 8
SSD_STATE = 128
SSD_CONV = 5
SSD_CHUNK = 128
MLA_HEADS = 32
MLA_NOPE = 128
MLA_ROPE = 64
MLA_V = 128
NA_HEADS = 32
NA_WIN_H = 8
NA_WIN_W = 16
NA_Q_ROWS = 8
NA_K_ROWS = 16
MASK_NEG = -1e30
LANES = 128
BF16_SUBLANES = 16
VMEM_LIMIT_BYTES = 56 * 2 ** 20
COND_ROWS = 16


def _cp(*sem):
    return pltpu.CompilerParams(dimension_semantics=sem, vmem_limit_bytes=VMEM_LIMIT_BYTES)


def _tile(n, pref):
    if n <= pref:
        return n
    t = (pref // LANES) * LANES
    while t >= LANES:
        if n % t == 0:
            return t
        t -= LANES
    raise ValueError(f"no lane-aligned tile for {n}")


def _silu(v):
    return v * jax.nn.sigmoid(v)


def _ada_kernel(c_ref, w1_ref, w2_ref, b_ref, o_ref):
    cnd = c_ref[...]
    t = jnp.dot(_silu(cnd).astype(BF16), w1_ref[0].astype(BF16), preferred_element_type=F32)
    o_ref[0] = jnp.dot(t.astype(BF16), w2_ref[0].astype(BF16), preferred_element_type=F32) + b_ref[0]


def _ada(cond, w1, w2, bias):
    depth, d, r = w1.shape
    n = w2.shape[-1]
    tn = _tile(n, 2048)
    return pl.pallas_call(
        _ada_kernel,
        grid=(depth, n // tn),
        in_specs=[pl.BlockSpec((COND_ROWS, d), lambda l, j: (0, 0)),
                  pl.BlockSpec((1, d, r), lambda l, j: (l, 0, 0)),
                  pl.BlockSpec((1, r, tn), lambda l, j: (l, 0, j)),
                  pl.BlockSpec((1, 1, tn), lambda l, j: (l, 0, j))],
        out_specs=pl.BlockSpec((1, COND_ROWS, tn), lambda l, j: (l, 0, j)),
        out_shape=jax.ShapeDtypeStruct((depth, COND_ROWS, n), F32),
        compiler_params=_cp("arbitrary", "arbitrary"),
        name="ada",
    )(cond, w1, w2, bias.reshape(depth, 1, n))


def _norm_kernel(x_ref, g_ref, *rest, modulated):
    o_ref = rest[-1]
    xv = x_ref[...].astype(F32)
    y = xv * lax.rsqrt(jnp.mean(xv * xv, axis=-1, keepdims=True) + NORM_EPS) * g_ref[...]
    if modulated:
        sh_ref, sc_ref = rest[0], rest[1]
        y = y * (1.0 + sc_ref[0]) + sh_ref[0]
    o_ref[...] = y.astype(o_ref.dtype)


def _norm_rows(xa, width, col_idx, gain, out_dtype, n_rows, tr, mods=None, which=None, seg=None):
    in_specs = [pl.BlockSpec((tr, width), lambda i: (i, col_idx)),
                pl.BlockSpec((1, width), lambda i: (0, 0))]
    args = [xa, gain.reshape(1, width).astype(F32)]
    if mods is not None:
        sh_w, sc_w = which
        in_specs += [pl.BlockSpec((1, 1, width), lambda i: (seg(i * tr) * N_MOD + sh_w, 0, 0)),
                     pl.BlockSpec((1, 1, width), lambda i: (seg(i * tr) * N_MOD + sc_w, 0, 0))]
        args += [mods, mods]
    return pl.pallas_call(
        functools.partial(_norm_kernel, modulated=mods is not None),
        grid=(n_rows // tr,),
        in_specs=in_specs,
        out_specs=pl.BlockSpec((tr, width), lambda i: (i, 0)),
        out_shape=jax.ShapeDtypeStruct((n_rows, width), out_dtype),
        compiler_params=_cp("parallel"),
        name="norm",
    )(*args)


def _mm_kernel(x_ref, w_ref, o_ref):
    o_ref[...] = jnp.dot(x_ref[...], w_ref[...], preferred_element_type=F32).astype(o_ref.dtype)


def _mm(xa, w, out_dtype, tm, tn):
    m, k = xa.shape
    n = w.shape[1]
    return pl.pallas_call(
        _mm_kernel,
        grid=(m // tm, n // tn),
        in_specs=[pl.BlockSpec((tm, k), lambda i, j: (i, 0)),
                  pl.BlockSpec((k, tn), lambda i, j: (0, j))],
        out_specs=pl.BlockSpec((tm, tn), lambda i, j: (i, j)),
        out_shape=jax.ShapeDtypeStruct((m, n), out_dtype),
        compiler_params=_cp("parallel", "arbitrary"),
        name="mm",
    )(xa, w)


def _mm_swiglu_kernel(x_ref, wg_ref, wu_ref, o_ref):
    xv = x_ref[...]
    gate = jnp.dot(xv, wg_ref[...], preferred_element_type=F32)
    up = jnp.dot(xv, wu_ref[...], preferred_element_type=F32)
    o_ref[...] = (_silu(gate) * up).astype(o_ref.dtype)


def _mm_swiglu(xa, wg, wu, tm, tn):
    m, k = xa.shape
    n = wg.shape[1]
    return pl.pallas_call(
        _mm_swiglu_kernel,
        grid=(m // tm, n // tn),
        in_specs=[pl.BlockSpec((tm, k), lambda i, j: (i, 0)),
                  pl.BlockSpec((k, tn), lambda i, j: (0, j)),
                  pl.BlockSpec((k, tn), lambda i, j: (0, j))],
        out_specs=pl.BlockSpec((tm, tn), lambda i, j: (i, j)),
        out_shape=jax.ShapeDtypeStruct((m, n), BF16),
        compiler_params=_cp("parallel", "arbitrary"),
        name="mm_swiglu",
    )(xa, wg, wu)


def _mm_res_kernel(*refs, n_pairs):
    o_ref = refs[-1]
    res_ref, gate_ref = refs[2 * n_pairs], refs[2 * n_pairs + 1]
    acc = None
    for i in range(n_pairs):
        part = jnp.dot(refs[i][...], refs[n_pairs + i][...], preferred_element_type=F32)
        acc = part if acc is None else acc + part
    o_ref[...] = res_ref[...] + gate_ref[0] * acc


def _mm_res(xas, w, res, mods, which, seg, tm, tn):
    m, n = res.shape
    n_pairs = len(xas)
    k = xas[0].shape[1]
    in_specs = [pl.BlockSpec((tm, k), lambda i, j: (i, 0)) for _ in xas]
    in_specs += [pl.BlockSpec((k, tn), functools.partial(lambda i, j, p: (p, j), p=p)) for p in range(n_pairs)]
    in_specs += [pl.BlockSpec((tm, tn), lambda i, j: (i, j)),
                 pl.BlockSpec((1, 1, tn), lambda i, j: (seg(i * tm) * N_MOD + which, 0, j))]
    return pl.pallas_call(
        functools.partial(_mm_res_kernel, n_pairs=n_pairs),
        grid=(m // tm, n // tn),
        in_specs=in_specs,
        out_specs=pl.BlockSpec((tm, tn), lambda i, j: (i, j)),
        out_shape=jax.ShapeDtypeStruct((m, n), F32),
        compiler_params=_cp("parallel", "arbitrary"),
        name="mm_res",
    )(*xas, *([w] * n_pairs), res, mods)


def _mm_qrope_kernel(x_ref, w_ref, cos_ref, sin_ref, o_ref, *, heads):
    acc = jnp.dot(x_ref[...], w_ref[...], preferred_element_type=F32)
    cs, sn = cos_ref[...], sin_ref[...]
    for hh in range(heads):
        base = hh * 2 * LANES
        o_ref[:, base:base + LANES] = acc[:, base:base + LANES].astype(o_ref.dtype)
        half = acc[:, base + LANES:base + 2 * LANES]
        rot = half * cs + pltpu.roll(half, MLA_ROPE, 1) * sn
        o_ref[:, base + LANES:base + 2 * LANES] = rot.astype(o_ref.dtype)


def _mm_qrope(xa, w, cos_t, sin_t, tm, heads_per_step, table_idx):
    m, k = xa.shape
    n = w.shape[1]
    tn = heads_per_step * 2 * LANES
    return pl.pallas_call(
        functools.partial(_mm_qrope_kernel, heads=heads_per_step),
        grid=(m // tm, n // tn),
        in_specs=[pl.BlockSpec((tm, k), lambda i, j: (i, 0)),
                  pl.BlockSpec((k, tn), lambda i, j: (0, j)),
                  pl.BlockSpec((tm, LANES), lambda i, j: (table_idx(i), 0)),
                  pl.BlockSpec((tm, LANES), lambda i, j: (table_idx(i), 0))],
        out_specs=pl.BlockSpec((tm, tn), lambda i, j: (i, j)),
        out_shape=jax.ShapeDtypeStruct((m, n), BF16),
        compiler_params=_cp("parallel", "arbitrary"),
        name="mm_qrope",
    )(xa, w, cos_t, sin_t)


def _conv_kernel(prev_ref, cur_ref, next_ref, w_ref, b_ref, o_ref, *, n_lat_blocks, blocks_per_seq, blocks_per_ctx):
    p = pl.program_id(0)
    is_ctx = p >= n_lat_blocks
    q = jnp.where(is_ctx, (p - n_lat_blocks) % blocks_per_ctx, p % blocks_per_seq)
    q_last = jnp.where(is_ctx, blocks_per_ctx - 1, blocks_per_seq - 1)
    keep_prev = jnp.where(q == 0, 0.0, 1.0)
    keep_next = jnp.where(q == q_last, 0.0, 1.0)
    xc = cur_ref[...].astype(F32)
    rows = xc.shape[0]
    halo = prev_ref.shape[0]
    full = jnp.concatenate([prev_ref[...].astype(F32) * keep_prev, xc,
                            next_ref[...].astype(F32) * keep_next], axis=0)
    total = rows + 2 * halo
    wv = w_ref[...]
    acc = jnp.zeros_like(xc) + b_ref[...]
    for tap in range(SSD_CONV):
        shift = (SSD_CONV // 2 - tap) % total
        moved = full if shift == 0 else pltpu.roll(full, shift, 0)
        acc = acc + wv[tap:tap + 1, :] * moved[halo:halo + rows]
    o_ref[...] = _silu(acc).astype(o_ref.dtype)


def _conv_silu(p1, col0, width, conv_w, conv_b, n_lat_rows, seq, ctx_len, rb, tc):
    m = p1.shape[0]
    halo = BF16_SUBLANES
    cb0 = col0 // tc
    per = rb // halo
    last_halo = m // halo - 1
    return pl.pallas_call(
        functools.partial(_conv_kernel, n_lat_blocks=n_lat_rows // rb, blocks_per_seq=seq // rb,
                          blocks_per_ctx=ctx_len // rb),
        grid=(m // rb, width // tc),
        in_specs=[pl.BlockSpec((halo, tc), lambda p, j: (jnp.maximum(p * per - 1, 0), cb0 + j)),
                  pl.BlockSpec((rb, tc), lambda p, j: (p, cb0 + j)),
                  pl.BlockSpec((halo, tc), lambda p, j: (jnp.minimum((p + 1) * per, last_halo), cb0 + j)),
                  pl.BlockSpec((SSD_CONV, tc), lambda p, j: (0, j)),
                  pl.BlockSpec((1, tc), lambda p, j: (0, j))],
        out_specs=pl.BlockSpec((rb, tc), lambda p, j: (p, j)),
        out_shape=jax.ShapeDtypeStruct((m, width), BF16),
        compiler_params=_cp("parallel", "arbitrary"),
        name="conv_silu",
    )(p1, p1, p1, jnp.transpose(conv_w).astype(F32), conv_b.reshape(1, width).astype(F32))


def _split3(v):
    hi = v.astype(BF16)
    r1 = v - hi.astype(F32)
    mid = r1.astype(BF16)
    lo = (r1 - mid.astype(F32)).astype(BF16)
    return hi, mid, lo


def _dot_exact_rhs(sel, v):
    return sum(jnp.dot(sel, part, preferred_element_type=F32) for part in _split3(v))


def _dot_exact_lhs(v, sel):
    return sum(jnp.dot(part, sel, preferred_element_type=F32) for part in _split3(v))


def _ssd_kernel(xs_ref, b_ref, c_ref, dt_ref, bias_ref, alog_ref, y_ref, st_ref, *, heads, heads_per_group):
    d = pl.program_id(1)
    g = pl.program_id(2)
    t, width = xs_ref.shape
    hd = SSD_HEAD_DIM

    @pl.when(pl.program_id(3) == 0)
    def _():
        st_ref[...] = jnp.zeros_like(st_ref)

    xv = dt_ref[...] + bias_ref[...]
    dtv = jnp.maximum(xv, 0.0) + jnp.log1p(jnp.exp(-jnp.abs(xv)))
    da = dtv * (-jnp.exp(alog_ref[...]))
    qi = lax.broadcasted_iota(jnp.int32, (t, t), 0)
    si = lax.broadcasted_iota(jnp.int32, (t, t), 1)
    mask = (si - qi) * jnp.where(d == 0, 1, -1) <= 0
    acum = _dot_exact_rhs(mask.astype(BF16), da)

    ncol = dt_ref.shape[1]
    col0 = d * heads + g * heads_per_group
    ri = lax.broadcasted_iota(jnp.int32, (ncol, width), 0)
    ci = lax.broadcasted_iota(jnp.int32, (ncol, width), 1)
    expand = (ri == col0 + ci // hd).astype(BF16)
    ri2 = lax.broadcasted_iota(jnp.int32, (ncol, LANES), 0)
    ci2 = lax.broadcasted_iota(jnp.int32, (ncol, LANES), 1)
    pick = jnp.logical_and(ri2 == col0 + ci2, ci2 < heads_per_group).astype(BF16)

    acum_x = _dot_exact_lhs(acum, expand)
    dtv_x = _dot_exact_lhs(dtv, expand)
    acum_t = jnp.transpose(_dot_exact_lhs(acum, pick))
    a_end = jnp.where(d == 0, acum_x[t - 1:t, :], acum_x[0:1, :])

    xdt = xs_ref[...].astype(F32) * dtv_x
    xdt_b = xdt.astype(BF16)
    scale_q = jnp.exp(acum_x)
    xw = (xdt * jnp.exp(a_end - acum_x)).astype(BF16)

    cg = c_ref[...]
    bg = b_ref[...]
    cb = lax.dot_general(cg, bg, (((1,), (1,)), ((), ())), preferred_element_type=F32)
    state = st_ref[...]
    y_state = jnp.dot(cg, state.astype(BF16), preferred_element_type=F32)
    bg_t = jnp.transpose(bg.astype(F32)).astype(BF16)
    st_ref[...] = state * jnp.exp(a_end) + jnp.dot(bg_t, xw, preferred_element_type=F32)

    lane = lax.broadcasted_iota(jnp.int32, (t, LANES), 1)
    for p in range(heads_per_group // 2):
        sl = slice(p * LANES, (p + 1) * LANES)
        mats = []
        for hh in (2 * p, 2 * p + 1):
            a_q = acum_x[:, hh * hd:hh * hd + 1]
            a_s = acum_t[hh:hh + 1, :]
            mats.append(cb * jnp.exp(jnp.where(mask, a_q - a_s, -jnp.inf)))
        lcat = jnp.concatenate(mats, axis=1).astype(BF16)
        xp = xdt_b[:, sl]
        xcat = jnp.concatenate([jnp.where(lane < hd, xp, jnp.zeros_like(xp)),
                                jnp.where(lane >= hd, xp, jnp.zeros_like(xp))], axis=0)
        y_in = jnp.dot(lcat, xcat, preferred_element_type=F32)
        y_ref[0, :, sl] = y_in + y_state[:, sl] * scale_q[:, sl]


def _ssd_scan(xbc, p2, dt_col_idx, dt_bias, a_log, batch, seq, ctx_len, d_inner):
    m = xbc.shape[0]
    t = SSD_CHUNK
    heads = d_inner // SSD_HEAD_DIM
    hpg = heads // SSD_GROUPS
    width = hpg * SSD_HEAD_DIM
    n = SSD_STATE
    nc_ctx, nc_lat = ctx_len // t, seq // t
    nch = nc_ctx + nc_lat
    ctx_base = batch * nc_lat
    ncol = _round_up(2 * heads, LANES)

    def row_block(b, d, i):
        ctx_blk = ctx_base + b * nc_ctx + jnp.where(d == 0, i, nc_ctx - 1 - i)
        lat_blk = b * nc_lat + jnp.where(d == 0, i - nc_ctx, nch - 1 - i)
        return jnp.where(i < nc_ctx, ctx_blk, lat_blk)

    b_col0 = d_inner // n
    c_col0 = b_col0 + SSD_GROUPS
    pad = ncol - 2 * heads
    bias = jnp.pad(dt_bias.reshape(1, 2 * heads).astype(F32), ((0, 0), (0, pad)))
    alog = jnp.pad(a_log.reshape(1, 2 * heads).astype(F32), ((0, 0), (0, pad)))
    return pl.pallas_call(
        functools.partial(_ssd_kernel, heads=heads, heads_per_group=hpg),
        grid=(batch, 2, SSD_GROUPS, nch),
        in_specs=[pl.BlockSpec((t, width), lambda b, d, g, i: (row_block(b, d, i), g)),
                  pl.BlockSpec((t, n), lambda b, d, g, i: (row_block(b, d, i), b_col0 + g)),
                  pl.BlockSpec((t, n), lambda b, d, g, i: (row_block(b, d, i), c_col0 + g)),
                  pl.BlockSpec((t, ncol), lambda b, d, g, i: (row_block(b, d, i), dt_col_idx)),
                  pl.BlockSpec((1, ncol), lambda b, d, g, i: (0, 0)),
                  pl.BlockSpec((1, ncol), lambda b, d, g, i: (0, 0))],
        out_specs=pl.BlockSpec((1, t, width), lambda b, d, g, i: (d, row_block(b, d, i), g)),
        out_shape=jax.ShapeDtypeStruct((2, m, d_inner), F32),
        scratch_shapes=[pltpu.VMEM((n, width), F32)],
        compiler_params=_cp("parallel", "parallel", "parallel", "arbitrary"),
        name="ssd_scan",
    )(xbc, xbc, xbc, p2, bias, alog)


def _round_up(v, mult):
    return -(-v // mult) * mult


def _ssd_finish_kernel(y_ref, xs_ref, z_ref, d_ref, g_ref, o_ref):
    y = y_ref[0] + y_ref[1] + d_ref[...] * xs_ref[...].astype(F32)
    zv = z_ref[...].astype(F32)
    y = y * _silu(zv)
    gw = y.shape[1] // SSD_GROUPS
    for gi in range(SSD_GROUPS):
        sl = slice(gi * gw, (gi + 1) * gw)
        yg = y[:, sl]
        inv = lax.rsqrt(jnp.mean(yg * yg, axis=-1, keepdims=True) + NORM_EPS)
        o_ref[:, sl] = (yg * inv * g_ref[:, sl]).astype(o_ref.dtype)


def _ssd_finish(y2, xbc, p1, d_skip, ssd_norm, d_inner, tr):
    m = xbc.shape[0]
    d_x = jnp.repeat(d_skip.astype(F32), SSD_HEAD_DIM).reshape(1, d_inner)
    return pl.pallas_call(
        _ssd_finish_kernel,
        grid=(m // tr,),
        in_specs=[pl.BlockSpec((2, tr, d_inner), lambda i: (0, i, 0)),
                  pl.BlockSpec((tr, d_inner), lambda i: (i, 0)),
                  pl.BlockSpec((tr, d_inner), lambda i: (i, 0)),
                  pl.BlockSpec((1, d_inner), lambda i: (0, 0)),
                  pl.BlockSpec((1, d_inner), lambda i: (0, 0))],
        out_specs=pl.BlockSpec((tr, d_inner), lambda i: (i, 0)),
        out_shape=jax.ShapeDtypeStruct((m, d_inner), BF16),
        compiler_params=_cp("parallel"),
        name="ssd_finish",
    )(y2, xbc, p1, d_x, ssd_norm.reshape(1, d_inner).astype(F32))


def _mla_attn_kernel(*refs, ctx_len, with_latent, scale):
    if with_latent:
        q_ref, kc_ref, vc_ref, pc_ref, kl_ref, vl_ref, pl_ref, cos_ref, sin_ref, _, o_ref, kf_ref = refs
    else:
        q_ref, kc_ref, vc_ref, pc_ref, _, o_ref, kf_ref = refs

    @pl.when(pl.program_id(2) == 0)
    def _():
        lane = lax.broadcasted_iota(jnp.int32, (ctx_len, LANES), 1)
        kf_ref[0:ctx_len, 0:LANES] = kc_ref[...]
        kf_ref[0:ctx_len, LANES:2 * LANES] = jnp.where(lane < MLA_ROPE, pc_ref[...], 0.0).astype(BF16)
        if with_latent:
            kf_ref[ctx_len:, 0:LANES] = kl_ref[...]
            blk = pl_ref[...]
            rot = blk * cos_ref[...] + pltpu.roll(blk, MLA_ROPE, 1) * sin_ref[...]
            kf_ref[ctx_len:, LANES:2 * LANES] = rot.astype(BF16)

    s = lax.dot_general(q_ref[...], kf_ref[...], (((1,), (1,)), ((), ())), preferred_element_type=F32) * scale
    mx = jnp.max(s, axis=-1, keepdims=True)
    p = jnp.exp(s - mx)
    denom = jnp.sum(p, axis=-1, keepdims=True)
    pb = p.astype(BF16)
    o = jnp.dot(pb[:, :ctx_len], vc_ref[...], preferred_element_type=F32)
    if with_latent:
        o = o + jnp.dot(pb[:, ctx_len:], vl_ref[...], preferred_element_type=F32)
    o_ref[...] = (o / denom).astype(o_ref.dtype)


def _mla_attention(qh, kv, p2, kpe_idx, cos_t, sin_t, batch, seq, ctx_len, tq):
    m = qh.shape[0]
    heads = MLA_HEADS
    scale = float((MLA_NOPE + MLA_ROPE) ** -0.5)
    cb = (batch * seq) // ctx_len
    nq = seq // tq
    out_shape = jax.ShapeDtypeStruct((m, heads * MLA_V), BF16)
    ctx_specs = [pl.BlockSpec((ctx_len, LANES), lambda b, h, i: (cb + b, 2 * h)),
                 pl.BlockSpec((ctx_len, LANES), lambda b, h, i: (cb + b, 2 * h + 1)),
                 pl.BlockSpec((ctx_len, LANES), lambda b, h, i: (cb + b, kpe_idx))]
    lat = pl.pallas_call(
        functools.partial(_mla_attn_kernel, ctx_len=ctx_len, with_latent=True, scale=scale),
        grid=(batch, heads, nq),
        in_specs=[pl.BlockSpec((tq, 2 * LANES), lambda b, h, i: (b * nq + i, h))] + ctx_specs + [
            pl.BlockSpec((seq, LANES), lambda b, h, i: (b, 2 * h)),
            pl.BlockSpec((seq, LANES), lambda b, h, i: (b, 2 * h + 1)),
            pl.BlockSpec((seq, LANES), lambda b, h, i: (b, kpe_idx)),
            pl.BlockSpec((seq, LANES), lambda b, h, i: (0, 0)),
            pl.BlockSpec((seq, LANES), lambda b, h, i: (0, 0)),
            pl.BlockSpec(memory_space=pl.ANY)],
        out_specs=pl.BlockSpec((tq, MLA_V), lambda b, h, i: (b * nq + i, h)),
        out_shape=out_shape,
        scratch_shapes=[pltpu.VMEM((ctx_len + seq, 2 * LANES), BF16)],
        input_output_aliases={9: 0},
        compiler_params=_cp("parallel", "parallel", "arbitrary"),
        name="mla_attn_latent",
    )(qh, kv, kv, p2, kv, kv, p2, cos_t, sin_t, jnp.zeros(out_shape.shape, BF16))
    return pl.pallas_call(
        functools.partial(_mla_attn_kernel, ctx_len=ctx_len, with_latent=False, scale=scale),
        grid=(batch, heads, 1),
        in_specs=[pl.BlockSpec((ctx_len, 2 * LANES), lambda b, h, i: (cb + b, h))] + ctx_specs + [
            pl.BlockSpec(memory_space=pl.ANY)],
        out_specs=pl.BlockSpec((ctx_len, MLA_V), lambda b, h, i: (cb + b, h)),
        out_shape=out_shape,
        scratch_shapes=[pltpu.VMEM((ctx_len, 2 * LANES), BF16)],
        input_output_aliases={4: 0},
        compiler_params=_cp("parallel", "parallel", "arbitrary"),
        name="mla_attn_ctx",
    )(qh, kv, kv, p2, lat)


def _na_kernel(q_ref, k_ref, v_ref, kc_ref, vc_ref, tab_ref, _, o_ref, *, grid_rows, scale):
    j = pl.program_id(2)
    ws = jnp.clip(NA_Q_ROWS * j - NA_WIN_H // 2, 0, grid_rows - NA_K_ROWS)
    start = pl.multiple_of(ws * GRID_W, GRID_W)
    n_keys = NA_K_ROWS * GRID_W
    kw = k_ref[pl.ds(start, n_keys), :]
    vw = v_ref[pl.ds(start, n_keys), :]
    qv = q_ref[...]
    nt = (((1,), (1,)), ((), ()))
    s = lax.dot_general(qv, kw, nt, preferred_element_type=F32) * scale + tab_ref[0, 0]
    sc = lax.dot_general(qv, kc_ref[...], nt, preferred_element_type=F32) * scale
    mx = jnp.maximum(jnp.max(s, axis=-1, keepdims=True), jnp.max(sc, axis=-1, keepdims=True))
    p = jnp.exp(s - mx)
    pc = jnp.exp(sc - mx)
    denom = jnp.sum(p, axis=-1, keepdims=True) + jnp.sum(pc, axis=-1, keepdims=True)
    o = (jnp.dot(p.astype(BF16), vw, preferred_element_type=F32)
         + jnp.dot(pc.astype(BF16), vc_ref[...], preferred_element_type=F32))
    o_ref[...] = (o / denom).astype(o_ref.dtype)


def _attn_small_kernel(q_ref, k_ref, v_ref, _, o_ref, *, scale):
    s = lax.dot_general(q_ref[...], k_ref[...], (((1,), (1,)), ((), ())), preferred_element_type=F32) * scale
    p = jnp.exp(s - jnp.max(s, axis=-1, keepdims=True))
    o = jnp.dot(p.astype(BF16), v_ref[...], preferred_element_type=F32)
    o_ref[...] = (o / jnp.sum(p, axis=-1, keepdims=True)).astype(o_ref.dtype)


def _na_bias_tables(rpb, grid_rows):
    n_blk = grid_rows // NA_Q_ROWS
    qa = np.arange(NA_Q_ROWS)[:, None, None, None]
    qc = np.arange(GRID_W)[None, :, None, None]
    km = np.arange(NA_K_ROWS)[None, None, :, None]
    kc = np.arange(GRID_W)[None, None, None, :]
    c0 = np.clip(qc - NA_WIN_W // 2, 0, GRID_W - NA_WIN_W)
    col_ok = (kc >= c0) & (kc < c0 + NA_WIN_W)
    dc = np.clip(kc - qc + NA_WIN_W - 1, 0, 2 * NA_WIN_W - 2)
    patterns = []
    for j in range(n_blk):
        ws = int(np.clip(NA_Q_ROWS * j - NA_WIN_H // 2, 0, grid_rows - NA_K_ROWS))
        qr = NA_Q_ROWS * j + qa
        kr = ws + km
        r0 = np.clip(qr - NA_WIN_H // 2, 0, grid_rows - NA_WIN_H)
        ok = (kr >= r0) & (kr < r0 + NA_WIN_H) & col_ok
        dr = np.clip(kr - qr + NA_WIN_H - 1, 0, 2 * NA_WIN_H - 2)
        idx = np.where(ok, dr * (2 * NA_WIN_W - 1) + dc, -1)
        patterns.append(np.broadcast_to(idx, (NA_Q_ROWS, GRID_W, NA_K_ROWS, GRID_W)).reshape(
            NA_Q_ROWS * GRID_W, NA_K_ROWS * GRID_W))
    cases = [patterns[0], patterns[min(1, n_blk - 1)], patterns[n_blk - 1]]
    for j in range(1, n_blk - 1):
        if not np.array_equal(patterns[j], cases[1]):
            raise ValueError("neighbourhood-attention middle blocks are not translation invariant")
    idx = np.stack(cases)
    flat = rpb.astype(F32).reshape(rpb.shape[0], -1)
    tab = jnp.take(flat, jnp.asarray(np.maximum(idx, 0)), axis=1)
    tab = jnp.where(jnp.asarray(idx >= 0)[None], tab, MASK_NEG)
    return jnp.transpose(tab, (1, 0, 2, 3))


def _na_attention(qkv, rpb, batch, seq, ctx_len, d_model):
    m = qkv.shape[0]
    heads = NA_HEADS
    hd = d_model // heads
    scale = float(hd ** -0.5)
    grid_rows = seq // GRID_W
    n_blk = grid_rows // NA_Q_ROWS
    tq = NA_Q_ROWS * GRID_W
    tk = NA_K_ROWS * GRID_W
    cb = (batch * seq) // ctx_len
    k0, v0 = d_model // hd, 2 * d_model // hd
    tab = _na_bias_tables(rpb, grid_rows)
    out_shape = jax.ShapeDtypeStruct((m, d_model), BF16)

    def case(j):
        return jnp.where(j == 0, 0, jnp.where(j == n_blk - 1, 2, 1))

    lat = pl.pallas_call(
        functools.partial(_na_kernel, grid_rows=grid_rows, scale=scale),
        grid=(heads, batch, n_blk),
        in_specs=[pl.BlockSpec((tq, hd), lambda h, b, j: (b * n_blk + j, h)),
                  pl.BlockSpec((seq, hd), lambda h, b, j: (b, k0 + h)),
                  pl.BlockSpec((seq, hd), lambda h, b, j: (b, v0 + h)),
                  pl.BlockSpec((ctx_len, hd), lambda h, b, j: (cb + b, k0 + h)),
                  pl.BlockSpec((ctx_len, hd), lambda h, b, j: (cb + b, v0 + h)),
                  pl.BlockSpec((1, 1, tq, tk), lambda h, b, j: (case(j), h, 0, 0)),
                  pl.BlockSpec(memory_space=pl.ANY)],
        out_specs=pl.BlockSpec((tq, hd), lambda h, b, j: (b * n_blk + j, h)),
        out_shape=out_shape,
        input_output_aliases={6: 0},
        compiler_params=_cp("parallel", "parallel", "arbitrary"),
        name="na_attn_latent",
    )(qkv, qkv, qkv, qkv, qkv, tab, jnp.zeros(out_shape.shape, BF16))
    return pl.pallas_call(
        functools.partial(_attn_small_kernel, scale=scale),
        grid=(batch, heads),
        in_specs=[pl.BlockSpec((ctx_len, hd), lambda b, h: (cb + b, h)),
                  pl.BlockSpec((ctx_len, hd), lambda b, h: (cb + b, k0 + h)),
                  pl.BlockSpec((ctx_len, hd), lambda b, h: (cb + b, v0 + h)),
                  pl.BlockSpec(memory_space=pl.ANY)],
        out_specs=pl.BlockSpec((ctx_len, hd), lambda b, h: (cb + b, h)),
        out_shape=out_shape,
        input_output_aliases={3: 0},
        compiler_params=_cp("parallel", "parallel"),
        name="na_attn_ctx",
    )(qkv, qkv, qkv, lat)


def _rope_tables(seq, pad_rows):
    pos = np.arange(seq)
    n_freq = MLA_ROPE // 4
    inv_freq = jnp.power(ROPE_THETA, -jnp.arange(n_freq, dtype=F32) / n_freq)
    rows = jnp.asarray(pos // GRID_W, F32)[:, None] * inv_freq
    cols = jnp.asarray(pos % GRID_W, F32)[:, None] * inv_freq
    ang = jnp.concatenate([rows, rows, cols, cols], axis=-1)
    sign = np.tile(np.concatenate([-np.ones(n_freq), np.ones(n_freq)]), 2).astype(np.float32)
    cos_t = jnp.concatenate([jnp.cos(ang), jnp.zeros((seq, LANES - MLA_ROPE), F32)], axis=1)
    sin_t = jnp.concatenate([jnp.sin(ang) * sign, jnp.zeros((seq, LANES - MLA_ROPE), F32)], axis=1)
    ident = jnp.concatenate([jnp.ones((pad_rows, MLA_ROPE), F32), jnp.zeros((pad_rows, LANES - MLA_ROPE), F32)], 1)
    return (jnp.concatenate([cos_t, ident], axis=0),
            jnp.concatenate([sin_t, jnp.zeros((pad_rows, LANES), F32)], axis=0))


def _rope_partner_perm():
    n_freq = MLA_ROPE // 4
    dd = np.arange(MLA_ROPE)
    return np.where(dd % (2 * n_freq) < n_freq, dd + n_freq, dd - n_freq)


def kernel(x, c, ctx, c_ctx, ada_w1, ada_w2, ada_b, norm_mix, norm_ffn, ffn_w_gate, ffn_w_up, ffn_w_down,
           hyb_w_in, ssd_conv_w, ssd_conv_b, ssd_dt_bias, ssd_a_log, ssd_d, ssd_norm,
           mla_q_norm, mla_w_q_up, mla_kv_norm, mla_w_kv_up, hyb_w_out,
           na_w_qkv, na_rpb, na_w_out, final_norm):
    batch, seq, d_model = x.shape
    ctx_len = ctx.shape[1]
    depth = ada_w1.shape[0]
    n_lat = batch * seq
    m = n_lat + batch * ctx_len
    d_inner = ssd_norm.shape[-1]
    heads = d_inner // SSD_HEAD_DIM
    conv_dim = ssd_conv_w.shape[1]
    q_rank = mla_q_norm.shape[-1]
    kv_rank = mla_kv_norm.shape[-1]
    tseg = int(np.gcd(seq, batch * ctx_len))
    tseg = min(tseg, 1024)
    tm_big = tseg
    tm_half = min(tseg, 512)
    tr = min(tseg, 256)

    def seg(row0):
        return jnp.where(row0 < n_lat, row0 // seq, batch)

    h = jnp.concatenate([x.reshape(n_lat, d_model), ctx.reshape(batch * ctx_len, d_model)], axis=0)
    cond = jnp.zeros((COND_ROWS, d_model), F32).at[:batch].set(c).at[batch].set(c_ctx)
    mods_all = _ada(cond, ada_w1, ada_w2, ada_b).reshape(depth, COND_ROWS * N_MOD, 1, d_model)

    dt_cols = _round_up(2 * heads, LANES)
    perm = _rope_partner_perm()
    cos_t, sin_t = _rope_tables(seq, tm_big)
    lat_tiles = n_lat // tm_big
    tiles_per_seq = seq // tm_big

    def rope_table_idx(i):
        return jnp.where(i < lat_tiles, i % tiles_per_seq, tiles_per_seq)

    for layer in range(depth):
        mods = mods_all[layer]
        u = _norm_rows(h, d_model, 0, norm_mix[layer], BF16, m, tr, mods, (0, 1), seg)
        if layer % 2 == 0:
            e = layer // 2
            w_in = hyb_w_in[e]
            s0 = d_inner + conv_dim
            s1 = s0 + 2 * heads
            s2 = s1 + q_rank
            s3 = s2 + kv_rank
            w_p1 = w_in[:, :s0].astype(BF16)
            w_kpe = w_in[:, s3:]
            w_p2 = jnp.concatenate([w_in[:, s1:s2], w_in[:, s2:s3],
                                    jnp.pad(w_in[:, s0:s1], ((0, 0), (0, dt_cols - 2 * heads))),
                                    w_kpe, w_kpe[:, perm]], axis=1).astype(BF16)
            kv_idx = q_rank // kv_rank
            dt_idx = (q_rank + kv_rank) // dt_cols
            kpe_idx = (q_rank + kv_rank + dt_cols) // LANES
            p1 = _mm(u, w_p1, BF16, tm_big, _tile(s0, 1024))
            p2 = _mm(u, w_p2, F32, tm_big, _tile(w_p2.shape[1], 1024))
            xbc = _conv_silu(p1, d_inner, conv_dim, ssd_conv_w[e], ssd_conv_b[e], n_lat, seq, ctx_len,
                             min(ctx_len, 256), _tile(conv_dim, 512))
            y2 = _ssd_scan(xbc, p2, dt_idx, ssd_dt_bias[e], ssd_a_log[e], batch, seq, ctx_len, d_inner)
            ssd_out = _ssd_finish(y2, xbc, p1, ssd_d[e], ssd_norm[e], d_inner, tr)

            qn = _norm_rows(p2, q_rank, 0, mla_q_norm[e], BF16, m, tr)
            kvn = _norm_rows(p2, kv_rank, kv_idx, mla_kv_norm[e], BF16, m, tr)
            wq = mla_w_q_up[e].reshape(q_rank, MLA_HEADS, MLA_NOPE + MLA_ROPE)
            wq_rope = wq[:, :, MLA_NOPE:]
            wq3 = jnp.concatenate([wq[:, :, :MLA_NOPE], wq_rope, wq_rope[:, :, perm]], axis=2)
            wq3 = wq3.reshape(q_rank, MLA_HEADS * 2 * LANES).astype(BF16)
            qh = _mm_qrope(qn, wq3, cos_t, sin_t, tm_big, min(4, MLA_HEADS), rope_table_idx)
            kv = _mm(kvn, mla_w_kv_up[e].astype(BF16), BF16, tm_big, _tile(MLA_HEADS * 2 * LANES, 2048))
            attn = _mla_attention(qh, kv, p2, kpe_idx, cos_t, sin_t, batch, seq, ctx_len, min(seq, 256))
            w_out = hyb_w_out[e].astype(BF16)
            h = _mm_res([ssd_out, attn], w_out, h, mods, 2, seg, tm_half, _tile(d_model, 512))
        else:
            o = layer // 2
            qkv = _mm(u, na_w_qkv[o].astype(BF16), BF16, tm_big, _tile(3 * d_model, 1024))
            att = _na_attention(qkv, na_rpb[o], batch, seq, ctx_len, d_model)
            h = _mm_res([att], na_w_out[o].astype(BF16), h, mods, 2, seg, tm_big, _tile(d_model, 512))
        u2 = _norm_rows(h, d_model, 0, norm_ffn[layer], BF16, m, tr, mods, (3, 4), seg)
        hidden = _mm_swiglu(u2, ffn_w_gate[layer].astype(BF16), ffn_w_up[layer].astype(BF16),
                            tm_big, _tile(ffn_w_gate.shape[-1], 256))
        h = _mm_res([hidden], ffn_w_down[layer].astype(BF16), h, mods, 5, seg, tm_half, _tile(d_model, 256))
    out = _norm_rows(h, d_model, 0, final_norm, F32, n_lat, tr)
    return out.reshape(batch, seq, d_model)
```

```python
import functools

import numpy as np
import jax
import jax.numpy as jnp
from jax import lax
from jax.experimental import pallas as pl
from jax.experimental.pallas import tpu as pltpu

F32 = jnp.float32
BF16 = jnp.bfloat16

NORM_EPS = 1e-6
N_MOD = 6
GRID_W = 64
ROPE_THETA = 10000.0
SSD_HEAD_DIM = 64
SSD_GROUPS = 8
SSD_STATE = 128
SSD_CONV = 5
SSD_CHUNK = 128
MLA_HEADS = 32
MLA_NOPE = 128
MLA_ROPE = 64
MLA_V = 128
NA_HEADS = 32
NA_WIN_H = 8
NA_WIN_W = 16
NA_Q_ROWS = 8
NA_K_ROWS = 16
MASK_NEG = -1e30
LOG2_E = 1.4426950408889634
MLA_KEY_CHUNK = 512
LANES = 128
BF16_SUBLANES = 16
VMEM_LIMIT_BYTES = 56 * 2 ** 20
COND_ROWS = 16


def _cp(*sem):
    return pltpu.CompilerParams(dimension_semantics=sem, vmem_limit_bytes=VMEM_LIMIT_BYTES)


def _tile(n, pref):
    if n <= pref:
        return n
    t = (pref // LANES) * LANES
    while t >= LANES:
        if n % t == 0:
            return t
        t -= LANES
    raise ValueError(f"no lane-aligned tile for {n}")


def _silu(v):
    return v * jax.nn.sigmoid(v)


def _ada_kernel(c_ref, w1_ref, w2_ref, b_ref, o_ref):
    cnd = c_ref[...]
    t = jnp.dot(_silu(cnd).astype(BF16), w1_ref[0].astype(BF16), preferred_element_type=F32)
    o_ref[0] = jnp.dot(t.astype(BF16), w2_ref[0].astype(BF16), preferred_element_type=F32) + b_ref[0]


def _ada(cond, w1, w2, bias):
    depth, d, r = w1.shape
    n = w2.shape[-1]
    tn = _tile(n, 2048)
    return pl.pallas_call(
        _ada_kernel,
        grid=(depth, n // tn),
        in_specs=[pl.BlockSpec((COND_ROWS, d), lambda l, j: (0, 0)),
                  pl.BlockSpec((1, d, r), lambda l, j: (l, 0, 0)),
                  pl.BlockSpec((1, r, tn), lambda l, j: (l, 0, j)),
                  pl.BlockSpec((1, 1, tn), lambda l, j: (l, 0, j))],
        out_specs=pl.BlockSpec((1, COND_ROWS, tn), lambda l, j: (l, 0, j)),
        out_shape=jax.ShapeDtypeStruct((depth, COND_ROWS, n), F32),
        compiler_params=_cp("arbitrary", "arbitrary"),
        name="ada",
    )(cond, w1, w2, bias.reshape(depth, 1, n))


def _norm_kernel(x_ref, g_ref, *rest, modulated):
    o_ref = rest[-1]
    xv = x_ref[...].astype(F32)
    y = xv * lax.rsqrt(jnp.mean(xv * xv, axis=-1, keepdims=True) + NORM_EPS) * g_ref[...]
    if modulated:
        sh_ref, sc_ref = rest[0], rest[1]
        y = y * (1.0 + sc_ref[0]) + sh_ref[0]
    o_ref[...] = y.astype(o_ref.dtype)


def _norm_rows(xa, width, col_idx, gain, out_dtype, n_rows, tr, mods=None, which=None, seg=None):
    in_specs = [pl.BlockSpec((tr, width), lambda i: (i, col_idx)),
                pl.BlockSpec((1, width), lambda i: (0, 0))]
    args = [xa, gain.reshape(1, width).astype(F32)]
    if mods is not None:
        sh_w, sc_w = which
        in_specs += [pl.BlockSpec((1, 1, width), lambda i: (seg(i * tr) * N_MOD + sh_w, 0, 0)),
                     pl.BlockSpec((1, 1, width), lambda i: (seg(i * tr) * N_MOD + sc_w, 0, 0))]
        args += [mods, mods]
    return pl.pallas_call(
        functools.partial(_norm_kernel, modulated=mods is not None),
        grid=(n_rows // tr,),
        in_specs=in_specs,
        out_specs=pl.BlockSpec((tr, width), lambda i: (i, 0)),
        out_shape=jax.ShapeDtypeStruct((n_rows, width), out_dtype),
        compiler_params=_cp("parallel"),
        name="norm",
    )(*args)


def _mm_kernel(x_ref, w_ref, o_ref, *, scaled_tiles, out_scale):
    acc = jnp.dot(x_ref[...], w_ref[...], preferred_element_type=F32)
    if scaled_tiles:
        acc = acc * jnp.where(pl.program_id(1) < scaled_tiles, out_scale, 1.0)
    o_ref[...] = acc.astype(o_ref.dtype)


def _mm(xa, w, out_dtype, tm, tn, scaled_cols=0, out_scale=1.0):
    m, k = xa.shape
    n = w.shape[1]
    return pl.pallas_call(
        functools.partial(_mm_kernel, scaled_tiles=scaled_cols // tn, out_scale=out_scale),
        grid=(m // tm, n // tn),
        in_specs=[pl.BlockSpec((tm, k), lambda i, j: (i, 0)),
                  pl.BlockSpec((k, tn), lambda i, j: (0, j))],
        out_specs=pl.BlockSpec((tm, tn), lambda i, j: (i, j)),
        out_shape=jax.ShapeDtypeStruct((m, n), out_dtype),
        compiler_params=_cp("parallel", "arbitrary"),
        name="mm",
    )(xa, w)


def _mm_swiglu_kernel(x_ref, wg_ref, wu_ref, o_ref):
    xv = x_ref[...]
    gate = jnp.dot(xv, wg_ref[...], preferred_element_type=F32)
    up = jnp.dot(xv, wu_ref[...], preferred_element_type=F32)
    o_ref[...] = (_silu(gate) * up).astype(o_ref.dtype)


def _mm_swiglu(xa, wg, wu, tm, tn):
    m, k = xa.shape
    n = wg.shape[1]
    return pl.pallas_call(
        _mm_swiglu_kernel,
        grid=(m // tm, n // tn),
        in_specs=[pl.BlockSpec((tm, k), lambda i, j: (i, 0)),
                  pl.BlockSpec((k, tn), lambda i, j: (0, j)),
                  pl.BlockSpec((k, tn), lambda i, j: (0, j))],
        out_specs=pl.BlockSpec((tm, tn), lambda i, j: (i, j)),
        out_shape=jax.ShapeDtypeStruct((m, n), BF16),
        compiler_params=_cp("parallel", "arbitrary"),
        name="mm_swiglu",
    )(xa, wg, wu)


def _mm_res_kernel(*refs, n_pairs):
    o_ref = refs[-1]
    res_ref, gate_ref = refs[2 * n_pairs], refs[2 * n_pairs + 1]
    acc = None
    for i in range(n_pairs):
        part = jnp.dot(refs[i][...], refs[n_pairs + i][...], preferred_element_type=F32)
        acc = part if acc is None else acc + part
    o_ref[...] = res_ref[...] + gate_ref[0] * acc


def _mm_res(xas, w, res, mods, which, seg, tm, tn):
    m, n = res.shape
    n_pairs = len(xas)
    k = xas[0].shape[1]
    in_specs = [pl.BlockSpec((tm, k), lambda i, j: (i, 0)) for _ in xas]
    in_specs += [pl.BlockSpec((k, tn), functools.partial(lambda i, j, p: (p, j), p=p)) for p in range(n_pairs)]
    in_specs += [pl.BlockSpec((tm, tn), lambda i, j: (i, j)),
                 pl.BlockSpec((1, 1, tn), lambda i, j: (seg(i * tm) * N_MOD + which, 0, j))]
    return pl.pallas_call(
        functools.partial(_mm_res_kernel, n_pairs=n_pairs),
        grid=(m // tm, n // tn),
        in_specs=in_specs,
        out_specs=pl.BlockSpec((tm, tn), lambda i, j: (i, j)),
        out_shape=jax.ShapeDtypeStruct((m, n), F32),
        compiler_params=_cp("parallel", "arbitrary"),
        name="mm_res",
    )(*xas, *([w] * n_pairs), res, mods)


def _mm_qrope_kernel(x_ref, w_ref, cos_ref, sin_ref, o_ref, *, heads, out_scale):
    acc = jnp.dot(x_ref[...], w_ref[...], preferred_element_type=F32) * out_scale
    cs, sn = cos_ref[...], sin_ref[...]
    for hh in range(heads):
        base = hh * 2 * LANES
        o_ref[:, base:base + LANES] = acc[:, base:base + LANES].astype(o_ref.dtype)
        half = acc[:, base + LANES:base + 2 * LANES]
        rot = half * cs + pltpu.roll(half, MLA_ROPE, 1) * sn
        o_ref[:, base + LANES:base + 2 * LANES] = rot.astype(o_ref.dtype)


def _mm_qrope(xa, w, cos_t, sin_t, tm, heads_per_step, table_idx):
    m, k = xa.shape
    n = w.shape[1]
    tn = heads_per_step * 2 * LANES
    out_scale = float((MLA_NOPE + MLA_ROPE) ** -0.5 * LOG2_E)
    return pl.pallas_call(
        functools.partial(_mm_qrope_kernel, heads=heads_per_step, out_scale=out_scale),
        grid=(m // tm, n // tn),
        in_specs=[pl.BlockSpec((tm, k), lambda i, j: (i, 0)),
                  pl.BlockSpec((k, tn), lambda i, j: (0, j)),
                  pl.BlockSpec((tm, LANES), lambda i, j: (table_idx(i), 0)),
                  pl.BlockSpec((tm, LANES), lambda i, j: (table_idx(i), 0))],
        out_specs=pl.BlockSpec((tm, tn), lambda i, j: (i, j)),
        out_shape=jax.ShapeDtypeStruct((m, n), BF16),
        compiler_params=_cp("parallel", "arbitrary"),
        name="mm_qrope",
    )(xa, w, cos_t, sin_t)


def _conv_kernel(prev_ref, cur_ref, next_ref, w_ref, b_ref, o_ref, *, n_lat_blocks, blocks_per_seq, blocks_per_ctx):
    p = pl.program_id(0)
    is_ctx = p >= n_lat_blocks
    q = jnp.where(is_ctx, (p - n_lat_blocks) % blocks_per_ctx, p % blocks_per_seq)
    q_last = jnp.where(is_ctx, blocks_per_ctx - 1, blocks_per_seq - 1)
    keep_prev = jnp.where(q == 0, 0.0, 1.0)
    keep_next = jnp.where(q == q_last, 0.0, 1.0)
    xc = cur_ref[...].astype(F32)
    rows = xc.shape[0]
    halo = prev_ref.shape[0]
    full = jnp.concatenate([prev_ref[...].astype(F32) * keep_prev, xc,
                            next_ref[...].astype(F32) * keep_next], axis=0)
    total = rows + 2 * halo
    wv = w_ref[...]
    acc = jnp.zeros_like(xc) + b_ref[...]
    for tap in range(SSD_CONV):
        shift = (SSD_CONV // 2 - tap) % total
        moved = full if shift == 0 else pltpu.roll(full, shift, 0)
        acc = acc + wv[tap:tap + 1, :] * moved[halo:halo + rows]
    o_ref[...] = _silu(acc).astype(o_ref.dtype)


def _conv_silu(p1, col0, width, conv_w, conv_b, n_lat_rows, seq, ctx_len, rb, tc):
    m = p1.shape[0]
    halo = BF16_SUBLANES
    cb0 = col0 // tc
    per = rb // halo
    last_halo = m // halo - 1
    return pl.pallas_call(
        functools.partial(_conv_kernel, n_lat_blocks=n_lat_rows // rb, blocks_per_seq=seq // rb,
                          blocks_per_ctx=ctx_len // rb),
        grid=(m // rb, width // tc),
        in_specs=[pl.BlockSpec((halo, tc), lambda p, j: (jnp.maximum(p * per - 1, 0), cb0 + j)),
                  pl.BlockSpec((rb, tc), lambda p, j: (p, cb0 + j)),
                  pl.BlockSpec((halo, tc), lambda p, j: (jnp.minimum((p + 1) * per, last_halo), cb0 + j)),
                  pl.BlockSpec((SSD_CONV, tc), lambda p, j: (0, j)),
                  pl.BlockSpec((1, tc), lambda p, j: (0, j))],
        out_specs=pl.BlockSpec((rb, tc), lambda p, j: (p, j)),
        out_shape=jax.ShapeDtypeStruct((m, width), BF16),
        compiler_params=_cp("parallel", "arbitrary"),
        name="conv_silu",
    )(p1, p1, p1, jnp.transpose(conv_w).astype(F32), conv_b.reshape(1, width).astype(F32))


def _split3(v):
    hi = v.astype(BF16)
    r1 = v - hi.astype(F32)
    mid = r1.astype(BF16)
    lo = (r1 - mid.astype(F32)).astype(BF16)
    return hi, mid, lo


def _dot_exact_rhs(sel, v):
    return sum(jnp.dot(sel, part, preferred_element_type=F32) for part in _split3(v))


def _dot_exact_lhs(v, sel):
    return sum(jnp.dot(part, sel, preferred_element_type=F32) for part in _split3(v))


def _ssd_kernel(xs_ref, b_ref, c_ref, dt_ref, bias_ref, alog_ref, y_ref, st_ref, ex_ref, *, heads,
                heads_per_group):
    d = pl.program_id(1)
    t = xs_ref.shape[0]
    hd = SSD_HEAD_DIM
    n = SSD_STATE
    width = heads_per_group * hd
    ncol = dt_ref.shape[1]

    @pl.when(pl.program_id(2) == 0)
    def _():
        st_ref[...] = jnp.zeros_like(st_ref)
        ri = lax.broadcasted_iota(jnp.int32, ex_ref.shape, 0)
        ci = lax.broadcasted_iota(jnp.int32, ex_ref.shape, 1)
        ex_ref[...] = (ri == d * heads + ci // hd).astype(BF16)

    xv = dt_ref[...] + bias_ref[...]
    dtv = jnp.maximum(xv, 0.0) + jnp.log1p(jnp.exp(-jnp.abs(xv)))
    da = dtv * (-jnp.exp(alog_ref[...]))
    qi = lax.broadcasted_iota(jnp.int32, (t, t), 0)
    si = lax.broadcasted_iota(jnp.int32, (t, t), 1)
    mask = (si - qi) * jnp.where(d == 0, 1, -1) <= 0
    acum = _dot_exact_rhs(mask.astype(BF16), da)

    expand = ex_ref[...]
    acum_x = _dot_exact_lhs(acum, expand)
    dtv_x = _dot_exact_lhs(dtv, expand)
    acum_t_all = jnp.transpose(acum)
    acum_t = jnp.where(d == 0, acum_t_all[0:heads], acum_t_all[heads:2 * heads])
    a_end = jnp.where(d == 0, acum_x[t - 1:t, :], acum_x[0:1, :])

    xdt = xs_ref[...].astype(F32) * dtv_x
    xdt_b = xdt.astype(BF16)
    scale_q = jnp.exp(acum_x)
    xw = (xdt * jnp.exp(a_end - acum_x)).astype(BF16)
    decay_end = jnp.exp(a_end)

    lane = lax.broadcasted_iota(jnp.int32, (t, LANES), 1)
    for g in range(SSD_GROUPS):
        gs = slice(g * width, (g + 1) * width)
        cg = c_ref[:, g * n:(g + 1) * n]
        bg = b_ref[:, g * n:(g + 1) * n]
        cb = lax.dot_general(cg, bg, (((1,), (1,)), ((), ())), preferred_element_type=F32)
        state = st_ref[g]
        y_state = jnp.dot(cg, state.astype(BF16), preferred_element_type=F32)
        bg_t = jnp.transpose(bg.astype(F32)).astype(BF16)
        st_ref[g] = state * decay_end[:, gs] + jnp.dot(bg_t, xw[:, gs], preferred_element_type=F32)
        for p in range(heads_per_group // 2):
            sl = slice(g * width + p * LANES, g * width + (p + 1) * LANES)
            mats = []
            for hh in (g * heads_per_group + 2 * p, g * heads_per_group + 2 * p + 1):
                a_q = acum_x[:, hh * hd:hh * hd + 1]
                a_s = acum_t[hh:hh + 1, :]
                mats.append(cb * jnp.exp(jnp.where(mask, a_q - a_s, -jnp.inf)))
            lcat = jnp.concatenate(mats, axis=1).astype(BF16)
            xp = xdt_b[:, sl]
            xcat = jnp.concatenate([jnp.where(lane < hd, xp, jnp.zeros_like(xp)),
                                    jnp.where(lane >= hd, xp, jnp.zeros_like(xp))], axis=0)
            y_in = jnp.dot(lcat, xcat, preferred_element_type=F32)
            y_ref[0, :, sl] = y_in + y_state[:, p * LANES:(p + 1) * LANES] * scale_q[:, sl]


def _ssd_scan(xbc, p2, dt_col_idx, dt_bias, a_log, batch, seq, ctx_len, d_inner):
    m = xbc.shape[0]
    t = SSD_CHUNK
    heads = d_inner // SSD_HEAD_DIM
    hpg = heads // SSD_GROUPS
    width = hpg * SSD_HEAD_DIM
    n = SSD_STATE
    nc_ctx, nc_lat = ctx_len // t, seq // t
    nch = nc_ctx + nc_lat
    ctx_base = batch * nc_lat
    ncol = _round_up(2 * heads, LANES)

    def row_block(b, d, i):
        ctx_blk = ctx_base + b * nc_ctx + jnp.where(d == 0, i, nc_ctx - 1 - i)
        lat_blk = b * nc_lat + jnp.where(d == 0, i - nc_ctx, nch - 1 - i)
        return jnp.where(i < nc_ctx, ctx_blk, lat_blk)

    gn = SSD_GROUPS * n
    b_col = d_inner // gn
    pad = ncol - 2 * heads
    bias = jnp.pad(dt_bias.reshape(1, 2 * heads).astype(F32), ((0, 0), (0, pad)))
    alog = jnp.pad(a_log.reshape(1, 2 * heads).astype(F32), ((0, 0), (0, pad)))
    return pl.pallas_call(
        functools.partial(_ssd_kernel, heads=heads, heads_per_group=hpg),
        grid=(batch, 2, nch),
        in_specs=[pl.BlockSpec((t, d_inner), lambda b, d, i: (row_block(b, d, i), 0)),
                  pl.BlockSpec((t, gn), lambda b, d, i: (row_block(b, d, i), b_col)),
                  pl.BlockSpec((t, gn), lambda b, d, i: (row_block(b, d, i), b_col + 1)),
                  pl.BlockSpec((t, ncol), lambda b, d, i: (row_block(b, d, i), dt_col_idx)),
                  pl.BlockSpec((1, ncol), lambda b, d, i: (0, 0)),
                  pl.BlockSpec((1, ncol), lambda b, d, i: (0, 0))],
        out_specs=pl.BlockSpec((1, t, d_inner), lambda b, d, i: (d, row_block(b, d, i), 0)),
        out_shape=jax.ShapeDtypeStruct((2, m, d_inner), F32),
        scratch_shapes=[pltpu.VMEM((SSD_GROUPS, n, width), F32), pltpu.VMEM((ncol, d_inner), BF16)],
        compiler_params=_cp("parallel", "parallel", "arbitrary"),
        name="ssd_scan",
    )(xbc, xbc, xbc, p2, bias, alog)


def _round_up(v, mult):
    return -(-v // mult) * mult


def _ssd_finish_kernel(y_ref, xs_ref, z_ref, d_ref, g_ref, o_ref):
    y = y_ref[0] + y_ref[1] + d_ref[...] * xs_ref[...].astype(F32)
    zv = z_ref[...].astype(F32)
    y = y * _silu(zv)
    gw = y.shape[1] // SSD_GROUPS
    for gi in range(SSD_GROUPS):
        sl = slice(gi * gw, (gi + 1) * gw)
        yg = y[:, sl]
        inv = lax.rsqrt(jnp.mean(yg * yg, axis=-1, keepdims=True) + NORM_EPS)
        o_ref[:, sl] = (yg * inv * g_ref[:, sl]).astype(o_ref.dtype)


def _ssd_finish(y2, xbc, p1, d_skip, ssd_norm, d_inner, tr):
    m = xbc.shape[0]
    d_x = jnp.repeat(d_skip.astype(F32), SSD_HEAD_DIM).reshape(1, d_inner)
    return pl.pallas_call(
        _ssd_finish_kernel,
        grid=(m // tr,),
        in_specs=[pl.BlockSpec((2, tr, d_inner), lambda i: (0, i, 0)),
                  pl.BlockSpec((tr, d_inner), lambda i: (i, 0)),
                  pl.BlockSpec((tr, d_inner), lambda i: (i, 0)),
                  pl.BlockSpec((1, d_inner), lambda i: (0, 0)),
                  pl.BlockSpec((1, d_inner), lambda i: (0, 0))],
        out_specs=pl.BlockSpec((tr, d_inner), lambda i: (i, 0)),
        out_shape=jax.ShapeDtypeStruct((m, d_inner), BF16),
        compiler_params=_cp("parallel"),
        name="ssd_finish",
    )(y2, xbc, p1, d_x, ssd_norm.reshape(1, d_inner).astype(F32))


def _mla_attn_kernel(*refs, ctx_len, with_latent, key_chunk):
    if with_latent:
        q_ref, kc_ref, vc_ref, pc_ref, kl_ref, vl_ref, pl_ref, cos_ref, sin_ref, _, o_ref, kt_ref, vs_ref = refs
        seq = kl_ref.shape[0]
    else:
        q_ref, kc_ref, vc_ref, pc_ref, _, o_ref, kt_ref, vs_ref = refs
        seq = 0

    @pl.when(pl.program_id(2) == 0)
    def _():
        lane = lax.broadcasted_iota(jnp.int32, (ctx_len, LANES), 1)
        kt_ref[0:LANES, 0:ctx_len] = jnp.transpose(kc_ref[...].astype(F32)).astype(BF16)
        kt_ref[LANES:, 0:ctx_len] = jnp.transpose(jnp.where(lane < MLA_ROPE, pc_ref[...], 0.0)).astype(BF16)
        vs_ref[:, LANES:] = jnp.ones((ctx_len + seq, LANES), BF16)
        vs_ref[0:ctx_len, 0:LANES] = vc_ref[...]
        for a in range(0, seq, key_chunk):
            rows = slice(a, a + key_chunk)
            cols = slice(ctx_len + a, ctx_len + a + key_chunk)
            kt_ref[0:LANES, cols] = jnp.transpose(kl_ref[rows, :].astype(F32)).astype(BF16)
            blk = pl_ref[rows, :]
            rot = blk * cos_ref[rows, :] + pltpu.roll(blk, MLA_ROPE, 1) * sin_ref[rows, :]
            kt_ref[LANES:, cols] = jnp.transpose(rot).astype(BF16)
            vs_ref[cols, 0:LANES] = vl_ref[rows, :]

    qv = q_ref[...]
    bounds = [(0, ctx_len)] + [(ctx_len + a, ctx_len + a + key_chunk) for a in range(0, seq, key_chunk)]
    m_run = acc = None
    for lo, hi in bounds:
        s = jnp.dot(qv, kt_ref[:, lo:hi], preferred_element_type=F32)
        m_new = jnp.max(s, axis=-1, keepdims=True)
        if m_run is not None:
            m_new = jnp.maximum(m_run, m_new)
        p = jnp.exp2(s - m_new)
        pv = jnp.dot(p.astype(BF16), vs_ref[lo:hi, :], preferred_element_type=F32)
        acc = pv if m_run is None else jnp.exp2(m_run - m_new) * acc + pv
        m_run = m_new
    o_ref[...] = (acc[:, :MLA_V] / acc[:, MLA_V:MLA_V + 1]).astype(o_ref.dtype)


def _mla_attention(qh, kv, p2, kpe_idx, cos_t, sin_t, batch, seq, ctx_len, tq):
    m = qh.shape[0]
    heads = MLA_HEADS
    key_chunk = min(seq, MLA_KEY_CHUNK)
    cb = (batch * seq) // ctx_len
    nq = seq // tq
    out_shape = jax.ShapeDtypeStruct((m, heads * MLA_V), BF16)
    ctx_specs = [pl.BlockSpec((ctx_len, LANES), lambda b, h, i: (cb + b, 2 * h)),
                 pl.BlockSpec((ctx_len, LANES), lambda b, h, i: (cb + b, 2 * h + 1)),
                 pl.BlockSpec((ctx_len, LANES), lambda b, h, i: (cb + b, kpe_idx))]
    lat = pl.pallas_call(
        functools.partial(_mla_attn_kernel, ctx_len=ctx_len, with_latent=True, key_chunk=key_chunk),
        grid=(batch, heads, nq),
        in_specs=[pl.BlockSpec((tq, 2 * LANES), lambda b, h, i: (b * nq + i, h))] + ctx_specs + [
            pl.BlockSpec((seq, LANES), lambda b, h, i: (b, 2 * h)),
            pl.BlockSpec((seq, LANES), lambda b, h, i: (b, 2 * h + 1)),
            pl.BlockSpec((seq, LANES), lambda b, h, i: (b, kpe_idx)),
            pl.BlockSpec((seq, LANES), lambda b, h, i: (0, 0)),
            pl.BlockSpec((seq, LANES), lambda b, h, i: (0, 0)),
            pl.BlockSpec(memory_space=pl.ANY)],
        out_specs=pl.BlockSpec((tq, MLA_V), lambda b, h, i: (b * nq + i, h)),
        out_shape=out_shape,
        scratch_shapes=[pltpu.VMEM((2 * LANES, ctx_len + seq), BF16),
                        pltpu.VMEM((ctx_len + seq, 2 * LANES), BF16)],
        input_output_aliases={9: 0},
        compiler_params=_cp("parallel", "parallel", "arbitrary"),
        name="mla_attn_latent",
    )(qh, kv, kv, p2, kv, kv, p2, cos_t, sin_t, jnp.zeros(out_shape.shape, BF16))
    return pl.pallas_call(
        functools.partial(_mla_attn_kernel, ctx_len=ctx_len, with_latent=False, key_chunk=key_chunk),
        grid=(batch, heads, 1),
        in_specs=[pl.BlockSpec((ctx_len, 2 * LANES), lambda b, h, i: (cb + b, h))] + ctx_specs + [
            pl.BlockSpec(memory_space=pl.ANY)],
        out_specs=pl.BlockSpec((ctx_len, MLA_V), lambda b, h, i: (cb + b, h)),
        out_shape=out_shape,
        scratch_shapes=[pltpu.VMEM((2 * LANES, ctx_len), BF16), pltpu.VMEM((ctx_len, 2 * LANES), BF16)],
        input_output_aliases={4: 0},
        compiler_params=_cp("parallel", "parallel", "arbitrary"),
        name="mla_attn_ctx",
    )(qh, kv, kv, p2, lat)


def _na_kernel(q_ref, k_ref, v_ref, kc_ref, vc_ref, pair_ref, _, o_ref, tab_ref, vs_ref, vcs_ref, *, grid_rows,
               n_blk, plan):
    j = pl.program_id(2)
    hd = v_ref.shape[1]

    @pl.when(j == 0)
    def _():
        vs_ref[:, 0:hd] = v_ref[...]
        vs_ref[:, hd:] = jnp.ones((v_ref.shape[0], hd), BF16)
        vcs_ref[:, 0:hd] = vc_ref[...]
        vcs_ref[:, hd:] = jnp.ones((vc_ref.shape[0], hd), BF16)

    @pl.when(jnp.logical_and(pl.program_id(1) == 0, j == 0))
    def _():
        for case, per_row in enumerate(plan):
            for qa, per_pair in enumerate(per_row):
                for i, src in enumerate(per_pair):
                    if src < 0:
                        blk = jnp.full((GRID_W, 2 * GRID_W), MASK_NEG, F32)
                    else:
                        blk = pair_ref[0, src] * LOG2_E
                    tab_ref[case, qa * GRID_W:(qa + 1) * GRID_W, i * 2 * GRID_W:(i + 1) * 2 * GRID_W] = blk

    case = jnp.where(j == 0, 0, jnp.where(j == n_blk - 1, 2, 1))
    ws = jnp.clip(NA_Q_ROWS * j - NA_WIN_H // 2, 0, grid_rows - NA_K_ROWS)
    start = pl.multiple_of(ws * GRID_W, GRID_W)
    n_keys = NA_K_ROWS * GRID_W
    kw = k_ref[pl.ds(start, n_keys), :]
    vw = vs_ref[pl.ds(start, n_keys), :]
    nt = (((1,), (1,)), ((), ()))
    half = q_ref.shape[0] // 2
    for part in range(2):
        rows = pl.ds(part * half, half)
        qv = q_ref[rows, :]
        s = lax.dot_general(qv, kw, nt, preferred_element_type=F32) + tab_ref[case, rows, :]
        sc = lax.dot_general(qv, kc_ref[...], nt, preferred_element_type=F32)
        mx = jnp.maximum(jnp.max(s, axis=-1, keepdims=True), jnp.max(sc, axis=-1, keepdims=True))
        p = jnp.exp2(s - mx).astype(BF16)
        pc = jnp.exp2(sc - mx).astype(BF16)
        o = (jnp.dot(p, vw, preferred_element_type=F32)
             + jnp.dot(pc, vcs_ref[...], preferred_element_type=F32))
        o_ref[rows, :] = (o[:, :hd] / o[:, hd:hd + 1]).astype(o_ref.dtype)


def _attn_small_kernel(q_ref, k_ref, v_ref, _, o_ref):
    s = lax.dot_general(q_ref[...], k_ref[...], (((1,), (1,)), ((), ())), preferred_element_type=F32)
    p = jnp.exp2(s - jnp.max(s, axis=-1, keepdims=True))
    o = jnp.dot(p.astype(BF16), v_ref[...], preferred_element_type=F32)
    o_ref[...] = (o / jnp.sum(p, axis=-1, keepdims=True)).astype(o_ref.dtype)


def _na_pair_blocks(rpb):
    qc = np.arange(GRID_W)[:, None]
    kc = np.arange(GRID_W)[None, :]
    c0 = np.clip(qc - NA_WIN_W // 2, 0, GRID_W - NA_WIN_W)
    col_ok = (kc >= c0) & (kc < c0 + NA_WIN_W)
    dc = np.clip(kc - qc + NA_WIN_W - 1, 0, 2 * NA_WIN_W - 2)
    a = jnp.take(rpb.astype(F32), jnp.asarray(dc), axis=2)
    a = jnp.where(jnp.asarray(col_ok), a, MASK_NEG)
    neg = jnp.full_like(a, MASK_NEG)
    return jnp.concatenate([jnp.concatenate([a[:, :-1], a[:, 1:]], axis=-1),
                            jnp.concatenate([a, neg], axis=-1),
                            jnp.concatenate([neg, a], axis=-1)], axis=1)


def _na_table_plan(grid_rows):
    n_blk = grid_rows // NA_Q_ROWS
    n_dr = 2 * NA_WIN_H - 1

    def block_plan(j):
        ws = int(np.clip(NA_Q_ROWS * j - NA_WIN_H // 2, 0, grid_rows - NA_K_ROWS))
        rows = []
        for qa in range(NA_Q_ROWS):
            qr = NA_Q_ROWS * j + qa
            r0 = int(np.clip(qr - NA_WIN_H // 2, 0, grid_rows - NA_WIN_H))
            pairs = []
            for i in range(NA_K_ROWS // 2):
                kr0 = ws + 2 * i
                ok0 = r0 <= kr0 < r0 + NA_WIN_H
                ok1 = r0 <= kr0 + 1 < r0 + NA_WIN_H
                dr0 = kr0 - qr + NA_WIN_H - 1
                if ok0 and ok1:
                    pairs.append(dr0)
                elif ok0:
                    pairs.append(n_dr - 1 + dr0)
                elif ok1:
                    pairs.append(2 * n_dr - 1 + dr0 + 1)
                else:
                    pairs.append(-1)
            rows.append(tuple(pairs))
        return tuple(rows)

    plans = [block_plan(j) for j in range(n_blk)]
    cases = (plans[0], plans[min(1, n_blk - 1)], plans[n_blk - 1])
    for j in range(1, n_blk - 1):
        if plans[j] != cases[1]:
            raise ValueError("neighbourhood-attention middle blocks are not translation invariant")
    return cases


def _na_attention(qkv, rpb, batch, seq, ctx_len, d_model):
    m = qkv.shape[0]
    heads = NA_HEADS
    hd = d_model // heads
    scale = float(hd ** -0.5)
    grid_rows = seq // GRID_W
    n_blk = grid_rows // NA_Q_ROWS
    tq = NA_Q_ROWS * GRID_W
    tk = NA_K_ROWS * GRID_W
    cb = (batch * seq) // ctx_len
    k0, v0 = d_model // hd, 2 * d_model // hd
    pairs = _na_pair_blocks(rpb)
    n_pairs = pairs.shape[1]
    out_shape = jax.ShapeDtypeStruct((m, d_model), BF16)

    lat = pl.pallas_call(
        functools.partial(_na_kernel, grid_rows=grid_rows, n_blk=n_blk, plan=_na_table_plan(grid_rows)),
        grid=(heads, batch, n_blk),
        in_specs=[pl.BlockSpec((tq, hd), lambda h, b, j: (b * n_blk + j, h)),
                  pl.BlockSpec((seq, hd), lambda h, b, j: (b, k0 + h)),
                  pl.BlockSpec((seq, hd), lambda h, b, j: (b, v0 + h)),
                  pl.BlockSpec((ctx_len, hd), lambda h, b, j: (cb + b, k0 + h)),
                  pl.BlockSpec((ctx_len, hd), lambda h, b, j: (cb + b, v0 + h)),
                  pl.BlockSpec((1, n_pairs, GRID_W, 2 * GRID_W), lambda h, b, j: (h, 0, 0, 0)),
                  pl.BlockSpec(memory_space=pl.ANY)],
        out_specs=pl.BlockSpec((tq, hd), lambda h, b, j: (b * n_blk + j, h)),
        out_shape=out_shape,
        scratch_shapes=[pltpu.VMEM((3, tq, tk), F32), pltpu.VMEM((seq, 2 * hd), BF16),
                        pltpu.VMEM((ctx_len, 2 * hd), BF16)],
        input_output_aliases={6: 0},
        compiler_params=_cp("arbitrary", "arbitrary", "arbitrary"),
        name="na_attn_latent",
    )(qkv, qkv, qkv, qkv, qkv, pairs, jnp.zeros(out_shape.shape, BF16))
    return pl.pallas_call(
        _attn_small_kernel,
        grid=(batch, heads),
        in_specs=[pl.BlockSpec((ctx_len, hd), lambda b, h: (cb + b, h)),
                  pl.BlockSpec((ctx_len, hd), lambda b, h: (cb + b, k0 + h)),
                  pl.BlockSpec((ctx_len, hd), lambda b, h: (cb + b, v0 + h)),
                  pl.BlockSpec(memory_space=pl.ANY)],
        out_specs=pl.BlockSpec((ctx_len, hd), lambda b, h: (cb + b, h)),
        out_shape=out_shape,
        input_output_aliases={3: 0},
        compiler_params=_cp("parallel", "parallel"),
        name="na_attn_ctx",
    )(qkv, qkv, qkv, lat)


def _rope_tables(seq, pad_rows):
    pos = np.arange(seq)
    n_freq = MLA_ROPE // 4
    inv_freq = jnp.power(ROPE_THETA, -jnp.arange(n_freq, dtype=F32) / n_freq)
    rows = jnp.asarray(pos // GRID_W, F32)[:, None] * inv_freq
    cols = jnp.asarray(pos % GRID_W, F32)[:, None] * inv_freq
    ang = jnp.concatenate([rows, rows, cols, cols], axis=-1)
    sign = np.tile(np.concatenate([-np.ones(n_freq), np.ones(n_freq)]), 2).astype(np.float32)
    cos_t = jnp.concatenate([jnp.cos(ang), jnp.zeros((seq, LANES - MLA_ROPE), F32)], axis=1)
    sin_t = jnp.concatenate([jnp.sin(ang) * sign, jnp.zeros((seq, LANES - MLA_ROPE), F32)], axis=1)
    ident = jnp.concatenate([jnp.ones((pad_rows, MLA_ROPE), F32), jnp.zeros((pad_rows, LANES - MLA_ROPE), F32)], 1)
    return (jnp.concatenate([cos_t, ident], axis=0),
            jnp.concatenate([sin_t, jnp.zeros((pad_rows, LANES), F32)], axis=0))


def _rope_partner_perm():
    n_freq = MLA_ROPE // 4
    dd = np.arange(MLA_ROPE)
    return np.where(dd % (2 * n_freq) < n_freq, dd + n_freq, dd - n_freq)


def kernel(x, c, ctx, c_ctx, ada_w1, ada_w2, ada_b, norm_mix, norm_ffn, ffn_w_gate, ffn_w_up, ffn_w_down,
           hyb_w_in, ssd_conv_w, ssd_conv_b, ssd_dt_bias, ssd_a_log, ssd_d, ssd_norm,
           mla_q_norm, mla_w_q_up, mla_kv_norm, mla_w_kv_up, hyb_w_out,
           na_w_qkv, na_rpb, na_w_out, final_norm):
    batch, seq, d_model = x.shape
    ctx_len = ctx.shape[1]
    depth = ada_w1.shape[0]
    n_lat = batch * seq
    m = n_lat + batch * ctx_len
    d_inner = ssd_norm.shape[-1]
    heads = d_inner // SSD_HEAD_DIM
    conv_dim = ssd_conv_w.shape[1]
    q_rank = mla_q_norm.shape[-1]
    kv_rank = mla_kv_norm.shape[-1]
    tseg = int(np.gcd(seq, batch * ctx_len))
    tseg = min(tseg, 1024)
    tm_big = tseg
    tm_half = min(tseg, 512)
    tr = min(tseg, 256)

    def seg(row0):
        return jnp.where(row0 < n_lat, row0 // seq, batch)

    h = jnp.concatenate([x.reshape(n_lat, d_model), ctx.reshape(batch * ctx_len, d_model)], axis=0)
    cond = jnp.zeros((COND_ROWS, d_model), F32).at[:batch].set(c).at[batch].set(c_ctx)
    mods_all = _ada(cond, ada_w1, ada_w2, ada_b).reshape(depth, COND_ROWS * N_MOD, 1, d_model)

    dt_cols = _round_up(2 * heads, LANES)
    perm = _rope_partner_perm()
    cos_t, sin_t = _rope_tables(seq, tm_big)
    lat_tiles = n_lat // tm_big
    tiles_per_seq = seq // tm_big

    def rope_table_idx(i):
        return jnp.where(i < lat_tiles, i % tiles_per_seq, tiles_per_seq)

    for layer in range(depth):
        mods = mods_all[layer]
        u = _norm_rows(h, d_model, 0, norm_mix[layer], BF16, m, tr, mods, (0, 1), seg)
        if layer % 2 == 0:
            e = layer // 2
            w_in = hyb_w_in[e]
            s0 = d_inner + conv_dim
            s1 = s0 + 2 * heads
            s2 = s1 + q_rank
            s3 = s2 + kv_rank
            w_p1 = w_in[:, :s0].astype(BF16)
            w_kpe = w_in[:, s3:]
            w_p2 = jnp.concatenate([w_in[:, s1:s2], w_in[:, s2:s3],
                                    jnp.pad(w_in[:, s0:s1], ((0, 0), (0, dt_cols - 2 * heads))),
                                    w_kpe, w_kpe[:, perm]], axis=1).astype(BF16)
            kv_idx = q_rank // kv_rank
            dt_idx = (q_rank + kv_rank) // dt_cols
            kpe_idx = (q_rank + kv_rank + dt_cols) // LANES
            p1 = _mm(u, w_p1, BF16, tm_big, _tile(s0, 1024))
            p2 = _mm(u, w_p2, F32, tm_big, _tile(w_p2.shape[1], 1024))
            xbc = _conv_silu(p1, d_inner, conv_dim, ssd_conv_w[e], ssd_conv_b[e], n_lat, seq, ctx_len,
                             min(ctx_len, 256), _tile(conv_dim, 512))
            y2 = _ssd_scan(xbc, p2, dt_idx, ssd_dt_bias[e], ssd_a_log[e], batch, seq, ctx_len, d_inner)
            ssd_out = _ssd_finish(y2, xbc, p1, ssd_d[e], ssd_norm[e], d_inner, tr)

            qn = _norm_rows(p2, q_rank, 0, mla_q_norm[e], BF16, m, tr)
            kvn = _norm_rows(p2, kv_rank, kv_idx, mla_kv_norm[e], BF16, m, tr)
            wq = mla_w_q_up[e].reshape(q_rank, MLA_HEADS, MLA_NOPE + MLA_ROPE)
            wq_rope = wq[:, :, MLA_NOPE:]
            wq3 = jnp.concatenate([wq[:, :, :MLA_NOPE], wq_rope, wq_rope[:, :, perm]], axis=2)
            wq3 = wq3.reshape(q_rank, MLA_HEADS * 2 * LANES).astype(BF16)
            qh = _mm_qrope(qn, wq3, cos_t, sin_t, tm_big, min(4, MLA_HEADS), rope_table_idx)
            kv = _mm(kvn, mla_w_kv_up[e].astype(BF16), BF16, tm_big, _tile(MLA_HEADS * 2 * LANES, 2048))
            attn = _mla_attention(qh, kv, p2, kpe_idx, cos_t, sin_t, batch, seq, ctx_len, min(seq, 512))
            w_out = hyb_w_out[e].astype(BF16)
            h = _mm_res([ssd_out, attn], w_out, h, mods, 2, seg, tm_half, _tile(d_model, 512))
        else:
            o = layer // 2
            qkv = _mm(u, na_w_qkv[o].astype(BF16), BF16, tm_big, _tile(d_model, 1024),
                      scaled_cols=d_model, out_scale=float((d_model // NA_HEADS) ** -0.5 * LOG2_E))
            att = _na_attention(qkv, na_rpb[o], batch, seq, ctx_len, d_model)
            h = _mm_res([att], na_w_out[o].astype(BF16), h, mods, 2, seg, tm_big, _tile(d_model, 512))
        u2 = _norm_rows(h, d_model, 0, norm_ffn[layer], BF16, m, tr, mods, (3, 4), seg)
        hidden = _mm_swiglu(u2, ffn_w_gate[layer].astype(BF16), ffn_w_up[layer].astype(BF16),
                            tm_big, _tile(ffn_w_gate.shape[-1], 256))
        h = _mm_res([hidden], ffn_w_down[layer].astype(BF16), h, mods, 5, seg, tm_half, _tile(d_model, 256))
    out = _norm_rows(h, d_model, 0, final_norm, F32, n_lat, tr)
    return out.reshape(batch, seq, d_model)
```

```python
import functools

import numpy as np
import jax
import jax.numpy as jnp
from jax import lax
from jax.experimental import pallas as pl
from jax.experimental.pallas import tpu as pltpu

F32 = jnp.float32
BF16 = jnp.bfloat16

NORM_EPS = 1e-6
N_MOD = 6
GRID_W = 64
ROPE_THETA = 10000.0
SSD_HEAD_DIM = 64
SSD_GROUPS = 8
SSD_STATE = 128
SSD_CONV = 5
SSD_CHUNK = 128
MLA_HEADS = 32
MLA_NOPE = 128
MLA_ROPE = 64
MLA_V = 128
NA_HEADS = 32
NA_WIN_H = 8
NA_WIN_W = 16
NA_Q_ROWS = 8
NA_K_ROWS = 16
NA_BLOCKS_PER_STEP = 2
NA_KEY_CHUNK = 256
MASK_NEG = -1e30
LOG2_E = 1.4426950408889634
MLA_KEY_CHUNK = 256
MLA_Q_TILE = 1024
LANES = 128
BF16_SUBLANES = 16
VMEM_LIMIT_BYTES = 56 * 2 ** 20
COND_ROWS = 16


def _cp(*sem):
    return pltpu.CompilerParams(dimension_semantics=sem, vmem_limit_bytes=VMEM_LIMIT_BYTES)


def _tile(n, pref):
    if n <= pref:
        return n
    t = (pref // LANES) * LANES
    while t >= LANES:
        if n % t == 0:
            return t
        t -= LANES
    raise ValueError(f"no lane-aligned tile for {n}")


def _silu(v):
    return v * jax.nn.sigmoid(v)


def _ada_kernel(c_ref, w1_ref, w2_ref, b_ref, o_ref):
    cnd = c_ref[...]
    t = jnp.dot(_silu(cnd).astype(BF16), w1_ref[0].astype(BF16), preferred_element_type=F32)
    o_ref[0] = jnp.dot(t.astype(BF16), w2_ref[0].astype(BF16), preferred_element_type=F32) + b_ref[0]


def _ada(cond, w1, w2, bias):
    depth, d, r = w1.shape
    n = w2.shape[-1]
    tn = _tile(n, 2048)
    return pl.pallas_call(
        _ada_kernel,
        grid=(depth, n // tn),
        in_specs=[pl.BlockSpec((COND_ROWS, d), lambda l, j: (0, 0)),
                  pl.BlockSpec((1, d, r), lambda l, j: (l, 0, 0)),
                  pl.BlockSpec((1, r, tn), lambda l, j: (l, 0, j)),
                  pl.BlockSpec((1, 1, tn), lambda l, j: (l, 0, j))],
        out_specs=pl.BlockSpec((1, COND_ROWS, tn), lambda l, j: (l, 0, j)),
        out_shape=jax.ShapeDtypeStruct((depth, COND_ROWS, n), F32),
        compiler_params=_cp("arbitrary", "arbitrary"),
        name="ada",
    )(cond, w1, w2, bias.reshape(depth, 1, n))


def _norm_kernel(x_ref, g_ref, *rest, modulated):
    o_ref = rest[-1]
    xv = x_ref[...].astype(F32)
    y = xv * lax.rsqrt(jnp.mean(xv * xv, axis=-1, keepdims=True) + NORM_EPS) * g_ref[...]
    if modulated:
        sh_ref, sc_ref = rest[0], rest[1]
        y = y * (1.0 + sc_ref[0]) + sh_ref[0]
    o_ref[...] = y.astype(o_ref.dtype)


def _norm_rows(xa, width, col_idx, gain, out_dtype, n_rows, tr, mods=None, which=None, seg=None):
    in_specs = [pl.BlockSpec((tr, width), lambda i: (i, col_idx)),
                pl.BlockSpec((1, width), lambda i: (0, 0))]
    args = [xa, gain.reshape(1, width).astype(F32)]
    if mods is not None:
        sh_w, sc_w = which
        in_specs += [pl.BlockSpec((1, 1, width), lambda i: (seg(i * tr) * N_MOD + sh_w, 0, 0)),
                     pl.BlockSpec((1, 1, width), lambda i: (seg(i * tr) * N_MOD + sc_w, 0, 0))]
        args += [mods, mods]
    return pl.pallas_call(
        functools.partial(_norm_kernel, modulated=mods is not None),
        grid=(n_rows // tr,),
        in_specs=in_specs,
        out_specs=pl.BlockSpec((tr, width), lambda i: (i, 0)),
        out_shape=jax.ShapeDtypeStruct((n_rows, width), out_dtype),
        compiler_params=_cp("parallel"),
        name="norm",
    )(*args)


def _w_spec(w, layer, k, tn):
    if w.ndim == 3:
        return pl.BlockSpec((None, k, tn), lambda i, j: (layer, 0, j))
    return pl.BlockSpec((k, tn), lambda i, j: (0, j))


def _mm_kernel(x_ref, w_ref, o_ref, *, scaled_tiles, out_scale):
    acc = jnp.dot(x_ref[...], w_ref[...].astype(BF16), preferred_element_type=F32)
    if scaled_tiles:
        acc = acc * jnp.where(pl.program_id(1) < scaled_tiles, out_scale, 1.0)
    o_ref[...] = acc.astype(o_ref.dtype)


def _mm(xa, w, out_dtype, tm, tn, layer=None, n=None, scaled_cols=0, out_scale=1.0):
    m, k = xa.shape
    n = w.shape[-1] if n is None else n
    return pl.pallas_call(
        functools.partial(_mm_kernel, scaled_tiles=scaled_cols // tn, out_scale=out_scale),
        grid=(m // tm, n // tn),
        in_specs=[pl.BlockSpec((tm, k), lambda i, j: (i, 0)), _w_spec(w, layer, k, tn)],
        out_specs=pl.BlockSpec((tm, tn), lambda i, j: (i, j)),
        out_shape=jax.ShapeDtypeStruct((m, n), out_dtype),
        compiler_params=_cp("parallel", "arbitrary"),
        name="mm",
    )(xa, w)


def _mm_swiglu_kernel(x_ref, wg_ref, wu_ref, o_ref):
    xv = x_ref[...]
    gate = jnp.dot(xv, wg_ref[...].astype(BF16), preferred_element_type=F32)
    up = jnp.dot(xv, wu_ref[...].astype(BF16), preferred_element_type=F32)
    o_ref[...] = (_silu(gate) * up).astype(o_ref.dtype)


def _mm_swiglu(xa, wg, wu, layer, rows, tm, tn):
    m, k = xa.shape
    n = wg.shape[-1]
    return pl.pallas_call(
        _mm_swiglu_kernel,
        grid=(rows // tm, n // tn),
        in_specs=[pl.BlockSpec((tm, k), lambda i, j: (i, 0)), _w_spec(wg, layer, k, tn), _w_spec(wu, layer, k, tn)],
        out_specs=pl.BlockSpec((tm, tn), lambda i, j: (i, j)),
        out_shape=jax.ShapeDtypeStruct((m, n), BF16),
        compiler_params=_cp("parallel", "arbitrary"),
        name="mm_swiglu",
    )(xa, wg, wu)


def _mm_res_kernel(*refs, n_pairs):
    o_ref = refs[-1]
    res_ref, gate_ref = refs[2 * n_pairs], refs[2 * n_pairs + 1]
    acc = None
    for i in range(n_pairs):
        part = jnp.dot(refs[i][...], refs[n_pairs + i][...], preferred_element_type=F32)
        acc = part if acc is None else acc + part
    o_ref[...] = res_ref[...] + gate_ref[0] * acc


def _mm_res(xas, w, res, mods, which, seg, rows, tm, tn):
    m, n = res.shape
    n_pairs = len(xas)
    k = xas[0].shape[1]
    in_specs = [pl.BlockSpec((tm, k), lambda i, j: (i, 0)) for _ in xas]
    in_specs += [pl.BlockSpec((k, tn), functools.partial(lambda i, j, p: (p, j), p=p)) for p in range(n_pairs)]
    in_specs += [pl.BlockSpec((tm, tn), lambda i, j: (i, j)),
                 pl.BlockSpec((1, 1, tn), lambda i, j: (seg(i * tm) * N_MOD + which, 0, j))]
    return pl.pallas_call(
        functools.partial(_mm_res_kernel, n_pairs=n_pairs),
        grid=(rows // tm, n // tn),
        in_specs=in_specs,
        out_specs=pl.BlockSpec((tm, tn), lambda i, j: (i, j)),
        out_shape=jax.ShapeDtypeStruct((m, n), F32),
        compiler_params=_cp("parallel", "arbitrary"),
        name="mm_res",
    )(*xas, *([w] * n_pairs), res, mods)


def _mm_qrope_kernel(x_ref, w_ref, cos_ref, sin_ref, o_ref, *, heads, out_scale):
    acc = jnp.dot(x_ref[...], w_ref[...], preferred_element_type=F32) * out_scale
    cs, sn = cos_ref[...], sin_ref[...]
    for hh in range(heads):
        base = hh * 2 * LANES
        o_ref[:, base:base + LANES] = acc[:, base:base + LANES].astype(o_ref.dtype)
        half = acc[:, base + LANES:base + 2 * LANES]
        rot = half * cs + pltpu.roll(half, MLA_ROPE, 1) * sn
        o_ref[:, base + LANES:base + 2 * LANES] = rot.astype(o_ref.dtype)


def _mm_qrope(xa, w, cos_t, sin_t, tm, heads_per_step, table_idx):
    m, k = xa.shape
    n = w.shape[1]
    tn = heads_per_step * 2 * LANES
    out_scale = float((MLA_NOPE + MLA_ROPE) ** -0.5 * LOG2_E)
    return pl.pallas_call(
        functools.partial(_mm_qrope_kernel, heads=heads_per_step, out_scale=out_scale),
        grid=(m // tm, n // tn),
        in_specs=[pl.BlockSpec((tm, k), lambda i, j: (i, 0)),
                  pl.BlockSpec((k, tn), lambda i, j: (0, j)),
                  pl.BlockSpec((tm, LANES), lambda i, j: (table_idx(i), 0)),
                  pl.BlockSpec((tm, LANES), lambda i, j: (table_idx(i), 0))],
        out_specs=pl.BlockSpec((tm, tn), lambda i, j: (i, j)),
        out_shape=jax.ShapeDtypeStruct((m, n), BF16),
        compiler_params=_cp("parallel", "arbitrary"),
        name="mm_qrope",
    )(xa, w, cos_t, sin_t)


def _conv_kernel(prev_ref, cur_ref, next_ref, w_ref, b_ref, o_ref, *, n_lat_blocks, blocks_per_seq, blocks_per_ctx):
    p = pl.program_id(0)
    is_ctx = p >= n_lat_blocks
    q = jnp.where(is_ctx, (p - n_lat_blocks) % blocks_per_ctx, p % blocks_per_seq)
    q_last = jnp.where(is_ctx, blocks_per_ctx - 1, blocks_per_seq - 1)
    keep_prev = jnp.where(q == 0, 0.0, 1.0)
    keep_next = jnp.where(q == q_last, 0.0, 1.0)
    xc = cur_ref[...].astype(F32)
    rows = xc.shape[0]
    halo = prev_ref.shape[0]
    full = jnp.concatenate([prev_ref[...].astype(F32) * keep_prev, xc,
                            next_ref[...].astype(F32) * keep_next], axis=0)
    total = rows + 2 * halo
    wv = w_ref[...]
    acc = jnp.zeros_like(xc) + b_ref[...]
    for tap in range(SSD_CONV):
        shift = (SSD_CONV // 2 - tap) % total
        moved = full if shift == 0 else pltpu.roll(full, shift, 0)
        acc = acc + wv[tap:tap + 1, :] * moved[halo:halo + rows]
    o_ref[...] = _silu(acc).astype(o_ref.dtype)


def _conv_silu(p1, col0, width, conv_w, conv_b, n_lat_rows, seq, ctx_len, rb, tc):
    m = p1.shape[0]
    halo = BF16_SUBLANES
    cb0 = col0 // tc
    per = rb // halo
    last_halo = m // halo - 1
    return pl.pallas_call(
        functools.partial(_conv_kernel, n_lat_blocks=n_lat_rows // rb, blocks_per_seq=seq // rb,
                          blocks_per_ctx=ctx_len // rb),
        grid=(m // rb, width // tc),
        in_specs=[pl.BlockSpec((halo, tc), lambda p, j: (jnp.maximum(p * per - 1, 0), cb0 + j)),
                  pl.BlockSpec((rb, tc), lambda p, j: (p, cb0 + j)),
                  pl.BlockSpec((halo, tc), lambda p, j: (jnp.minimum((p + 1) * per, last_halo), cb0 + j)),
                  pl.BlockSpec((SSD_CONV, tc), lambda p, j: (0, j)),
                  pl.BlockSpec((1, tc), lambda p, j: (0, j))],
        out_specs=pl.BlockSpec((rb, tc), lambda p, j: (p, j)),
        out_shape=jax.ShapeDtypeStruct((m, width), BF16),
        compiler_params=_cp("parallel", "arbitrary"),
        name="conv_silu",
    )(p1, p1, p1, jnp.transpose(conv_w).astype(F32), conv_b.reshape(1, width).astype(F32))


def _split3(v):
    hi = v.astype(BF16)
    r1 = v - hi.astype(F32)
    mid = r1.astype(BF16)
    lo = (r1 - mid.astype(F32)).astype(BF16)
    return hi, mid, lo


def _dot_exact_rhs(sel, v):
    return sum(jnp.dot(sel, part, preferred_element_type=F32) for part in _split3(v))


def _dot_exact_lhs(v, sel):
    return sum(jnp.dot(part, sel, preferred_element_type=F32) for part in _split3(v))


def _ssd_kernel(xs_ref, b_ref, c_ref, dt_ref, bias_ref, alog_ref, y_ref, st_ref, ex_ref, *, heads,
                heads_per_group):
    d = pl.program_id(1)
    t = xs_ref.shape[0]
    hd = SSD_HEAD_DIM
    n = SSD_STATE
    width = heads_per_group * hd
    ncol = dt_ref.shape[1]

    @pl.when(pl.program_id(2) == 0)
    def _():
        st_ref[...] = jnp.zeros_like(st_ref)
        ri = lax.broadcasted_iota(jnp.int32, ex_ref.shape, 0)
        ci = lax.broadcasted_iota(jnp.int32, ex_ref.shape, 1)
        ex_ref[...] = (ri == d * heads + ci // hd).astype(BF16)

    xv = dt_ref[...] + bias_ref[...]
    dtv = jnp.maximum(xv, 0.0) + jnp.log1p(jnp.exp(-jnp.abs(xv)))
    da = dtv * (-jnp.exp(alog_ref[...]))
    qi = lax.broadcasted_iota(jnp.int32, (t, t), 0)
    si = lax.broadcasted_iota(jnp.int32, (t, t), 1)
    mask = (si - qi) * jnp.where(d == 0, 1, -1) <= 0
    acum = _dot_exact_rhs(mask.astype(BF16), da)

    expand = ex_ref[...]
    acum_x = _dot_exact_lhs(acum, expand)
    dtv_x = _dot_exact_lhs(dtv, expand)
    acum_t_all = jnp.transpose(acum)
    acum_t = jnp.where(d == 0, acum_t_all[0:heads], acum_t_all[heads:2 * heads])
    a_end = jnp.where(d == 0, acum_x[t - 1:t, :], acum_x[0:1, :])

    xdt = xs_ref[...].astype(F32) * dtv_x
    xdt_b = xdt.astype(BF16)
    scale_q = jnp.exp(acum_x)
    xw = (xdt * jnp.exp(a_end - acum_x)).astype(BF16)
    decay_end = jnp.exp(a_end)

    lane = lax.broadcasted_iota(jnp.int32, (t, LANES), 1)
    for g in range(SSD_GROUPS):
        gs = slice(g * width, (g + 1) * width)
        cg = c_ref[:, g * n:(g + 1) * n]
        bg = b_ref[:, g * n:(g + 1) * n]
        cb = lax.dot_general(cg, bg, (((1,), (1,)), ((), ())), preferred_element_type=F32)
        state = st_ref[g]
        y_state = jnp.dot(cg, state.astype(BF16), preferred_element_type=F32)
        bg_t = jnp.transpose(bg.astype(F32)).astype(BF16)
        st_ref[g] = state * decay_end[:, gs] + jnp.dot(bg_t, xw[:, gs], preferred_element_type=F32)
        for p in range(heads_per_group // 2):
            sl = slice(g * width + p * LANES, g * width + (p + 1) * LANES)
            mats = []
            for hh in (g * heads_per_group + 2 * p, g * heads_per_group + 2 * p + 1):
                a_q = acum_x[:, hh * hd:hh * hd + 1]
                a_s = acum_t[hh:hh + 1, :]
                mats.append(cb * jnp.exp(jnp.where(mask, a_q - a_s, -jnp.inf)))
            lcat = jnp.concatenate(mats, axis=1).astype(BF16)
            xp = xdt_b[:, sl]
            xcat = jnp.concatenate([jnp.where(lane < hd, xp, jnp.zeros_like(xp)),
                                    jnp.where(lane >= hd, xp, jnp.zeros_like(xp))], axis=0)
            y_in = jnp.dot(lcat, xcat, preferred_element_type=F32)
            y_ref[0, :, sl] = y_in + y_state[:, p * LANES:(p + 1) * LANES] * scale_q[:, sl]


def _ssd_scan(xbc, p2, dt_col_idx, dt_bias, a_log, batch, seq, ctx_len, d_inner):
    m = xbc.shape[0]
    t = SSD_CHUNK
    heads = d_inner // SSD_HEAD_DIM
    hpg = heads // SSD_GROUPS
    width = hpg * SSD_HEAD_DIM
    n = SSD_STATE
    nc_ctx, nc_lat = ctx_len // t, seq // t
    nch = nc_ctx + nc_lat
    ctx_base = batch * nc_lat
    ncol = _round_up(2 * heads, LANES)

    def row_block(b, d, i):
        ctx_blk = ctx_base + b * nc_ctx + jnp.where(d == 0, i, nc_ctx - 1 - i)
        lat_blk = b * nc_lat + jnp.where(d == 0, i - nc_ctx, nch - 1 - i)
        return jnp.where(i < nc_ctx, ctx_blk, lat_blk)

    gn = SSD_GROUPS * n
    b_col = d_inner // gn
    pad = ncol - 2 * heads
    bias = jnp.pad(dt_bias.reshape(1, 2 * heads).astype(F32), ((0, 0), (0, pad)))
    alog = jnp.pad(a_log.reshape(1, 2 * heads).astype(F32), ((0, 0), (0, pad)))
    return pl.pallas_call(
        functools.partial(_ssd_kernel, heads=heads, heads_per_group=hpg),
        grid=(batch, 2, nch),
        in_specs=[pl.BlockSpec((t, d_inner), lambda b, d, i: (row_block(b, d, i), 0)),
                  pl.BlockSpec((t, gn), lambda b, d, i: (row_block(b, d, i), b_col)),
                  pl.BlockSpec((t, gn), lambda b, d, i: (row_block(b, d, i), b_col + 1)),
                  pl.BlockSpec((t, ncol), lambda b, d, i: (row_block(b, d, i), dt_col_idx)),
                  pl.BlockSpec((1, ncol), lambda b, d, i: (0, 0)),
                  pl.BlockSpec((1, ncol), lambda b, d, i: (0, 0))],
        out_specs=pl.BlockSpec((1, t, d_inner), lambda b, d, i: (d, row_block(b, d, i), 0)),
        out_shape=jax.ShapeDtypeStruct((2, m, d_inner), F32),
        scratch_shapes=[pltpu.VMEM((SSD_GROUPS, n, width), F32), pltpu.VMEM((ncol, d_inner), BF16)],
        compiler_params=_cp("parallel", "parallel", "arbitrary"),
        name="ssd_scan",
    )(xbc, xbc, xbc, p2, bias, alog)


def _round_up(v, mult):
    return -(-v // mult) * mult


def _ssd_finish_kernel(y_ref, xs_ref, z_ref, d_ref, g_ref, o_ref):
    y = y_ref[0] + y_ref[1] + d_ref[...] * xs_ref[...].astype(F32)
    zv = z_ref[...].astype(F32)
    y = y * _silu(zv)
    gw = y.shape[1] // SSD_GROUPS
    for gi in range(SSD_GROUPS):
        sl = slice(gi * gw, (gi + 1) * gw)
        yg = y[:, sl]
        inv = lax.rsqrt(jnp.mean(yg * yg, axis=-1, keepdims=True) + NORM_EPS)
        o_ref[:, sl] = (yg * inv * g_ref[:, sl]).astype(o_ref.dtype)


def _ssd_finish(y2, xbc, p1, d_skip, ssd_norm, d_inner, tr):
    m = xbc.shape[0]
    d_x = jnp.repeat(d_skip.astype(F32), SSD_HEAD_DIM).reshape(1, d_inner)
    return pl.pallas_call(
        _ssd_finish_kernel,
        grid=(m // tr,),
        in_specs=[pl.BlockSpec((2, tr, d_inner), lambda i: (0, i, 0)),
                  pl.BlockSpec((tr, d_inner), lambda i: (i, 0)),
                  pl.BlockSpec((tr, d_inner), lambda i: (i, 0)),
                  pl.BlockSpec((1, d_inner), lambda i: (0, 0)),
                  pl.BlockSpec((1, d_inner), lambda i: (0, 0))],
        out_specs=pl.BlockSpec((tr, d_inner), lambda i: (i, 0)),
        out_shape=jax.ShapeDtypeStruct((m, d_inner), BF16),
        compiler_params=_cp("parallel"),
        name="ssd_finish",
    )(y2, xbc, p1, d_x, ssd_norm.reshape(1, d_inner).astype(F32))


def _mla_attn_kernel(*refs, ctx_len, with_latent, key_chunk):
    if with_latent:
        q_ref, kc_ref, vc_ref, pc_ref, kl_ref, vl_ref, pl_ref, cos_ref, sin_ref, _, o_ref, kt_ref, vs_ref = refs
        seq = kl_ref.shape[0]
    else:
        q_ref, kc_ref, vc_ref, pc_ref, _, o_ref, kt_ref, vs_ref = refs
        seq = 0

    @pl.when(pl.program_id(2) == 0)
    def _():
        lane = lax.broadcasted_iota(jnp.int32, (ctx_len, LANES), 1)
        kt_ref[0:LANES, 0:ctx_len] = jnp.transpose(kc_ref[...].astype(F32)).astype(BF16)
        kt_ref[LANES:, 0:ctx_len] = jnp.transpose(jnp.where(lane < MLA_ROPE, pc_ref[...], 0.0)).astype(BF16)
        vs_ref[0:ctx_len, :] = vc_ref[...]
        for a in range(0, seq, key_chunk):
            rows = slice(a, a + key_chunk)
            cols = slice(ctx_len + a, ctx_len + a + key_chunk)
            kt_ref[0:LANES, cols] = jnp.transpose(kl_ref[rows, :].astype(F32)).astype(BF16)
            blk = pl_ref[rows, :]
            rot = blk * cos_ref[rows, :] + pltpu.roll(blk, MLA_ROPE, 1) * sin_ref[rows, :]
            kt_ref[LANES:, cols] = jnp.transpose(rot).astype(BF16)
            vs_ref[cols, :] = vl_ref[rows, :]

    qv = q_ref[...]
    bounds = [(0, ctx_len)] + [(ctx_len + a, ctx_len + a + key_chunk) for a in range(0, seq, key_chunk)]
    m_run = acc = l_part = None
    for lo, hi in bounds:
        s = jnp.dot(qv, kt_ref[:, lo:hi], preferred_element_type=F32)
        m_new = jnp.max(s, axis=-1, keepdims=True)
        if m_run is not None:
            m_new = jnp.maximum(m_run, m_new)
        p = jnp.exp2(s - m_new)
        l_new = p[:, 0:LANES]
        for c in range(LANES, hi - lo, LANES):
            l_new = l_new + p[:, c:c + LANES]
        pv = jnp.dot(p.astype(BF16), vs_ref[lo:hi, :], preferred_element_type=F32)
        if m_run is None:
            acc, l_part = pv, l_new
        else:
            alpha = jnp.exp2(m_run - m_new)
            acc = alpha * acc + pv
            l_part = alpha * l_part + l_new
        m_run = m_new
    o_ref[...] = (acc / jnp.sum(l_part, axis=-1, keepdims=True)).astype(o_ref.dtype)


def _mla_attention(qh, kv, p2, kpe_idx, cos_t, sin_t, batch, seq, ctx_len, tq):
    m = qh.shape[0]
    heads = MLA_HEADS
    key_chunk = min(seq, MLA_KEY_CHUNK)
    cb = (batch * seq) // ctx_len
    nq = seq // tq
    out_shape = jax.ShapeDtypeStruct((m, heads * MLA_V), BF16)
    ctx_specs = [pl.BlockSpec((ctx_len, LANES), lambda b, h, i: (cb + b, 2 * h)),
                 pl.BlockSpec((ctx_len, LANES), lambda b, h, i: (cb + b, 2 * h + 1)),
                 pl.BlockSpec((ctx_len, LANES), lambda b, h, i: (cb + b, kpe_idx))]
    lat = pl.pallas_call(
        functools.partial(_mla_attn_kernel, ctx_len=ctx_len, with_latent=True, key_chunk=key_chunk),
        grid=(batch, heads, nq),
        in_specs=[pl.BlockSpec((tq, 2 * LANES), lambda b, h, i: (b * nq + i, h))] + ctx_specs + [
            pl.BlockSpec((seq, LANES), lambda b, h, i: (b, 2 * h)),
            pl.BlockSpec((seq, LANES), lambda b, h, i: (b, 2 * h + 1)),
            pl.BlockSpec((seq, LANES), lambda b, h, i: (b, kpe_idx)),
            pl.BlockSpec((seq, LANES), lambda b, h, i: (0, 0)),
            pl.BlockSpec((seq, LANES), lambda b, h, i: (0, 0)),
            pl.BlockSpec(memory_space=pl.ANY)],
        out_specs=pl.BlockSpec((tq, MLA_V), lambda b, h, i: (b * nq + i, h)),
        out_shape=out_shape,
        scratch_shapes=[pltpu.VMEM((2 * LANES, ctx_len + seq), BF16),
                        pltpu.VMEM((ctx_len + seq, MLA_V), BF16)],
        input_output_aliases={9: 0},
        compiler_params=_cp("parallel", "parallel", "arbitrary"),
        name="mla_attn_latent",
    )(qh, kv, kv, p2, kv, kv, p2, cos_t, sin_t, jnp.zeros(out_shape.shape, BF16))
    return pl.pallas_call(
        functools.partial(_mla_attn_kernel, ctx_len=ctx_len, with_latent=False, key_chunk=key_chunk),
        grid=(batch, heads, 1),
        in_specs=[pl.BlockSpec((ctx_len, 2 * LANES), lambda b, h, i: (cb + b, h))] + ctx_specs + [
            pl.BlockSpec(memory_space=pl.ANY)],
        out_specs=pl.BlockSpec((ctx_len, MLA_V), lambda b, h, i: (cb + b, h)),
        out_shape=out_shape,
        scratch_shapes=[pltpu.VMEM((2 * LANES, ctx_len), BF16), pltpu.VMEM((ctx_len, MLA_V), BF16)],
        input_output_aliases={4: 0},
        compiler_params=_cp("parallel", "parallel", "arbitrary"),
        name="mla_attn_ctx",
    )(qh, kv, kv, p2, lat)


def _na_kernel(q_ref, k_ref, v_ref, kc_ref, vc_ref, pair_ref, _, o_ref, tab_ref, *, grid_rows, n_blk,
               blocks_per_step, key_chunk, plan):
    step = pl.program_id(2)

    @pl.when(jnp.logical_and(pl.program_id(1) == 0, step == 0))
    def _():
        for case, per_row in enumerate(plan):
            for qa, per_pair in enumerate(per_row):
                for i, src in enumerate(per_pair):
                    if src < 0:
                        blk = jnp.full((GRID_W, 2 * GRID_W), MASK_NEG, F32)
                    else:
                        blk = pair_ref[0, src] * LOG2_E
                    tab_ref[case, qa * GRID_W:(qa + 1) * GRID_W, i * 2 * GRID_W:(i + 1) * 2 * GRID_W] = blk

    nt = (((1,), (1,)), ((), ()))
    tq = NA_Q_ROWS * GRID_W
    n_keys = NA_K_ROWS * GRID_W
    for sub in range(blocks_per_step):
        j = step * blocks_per_step + sub
        case = jnp.where(j == 0, 0, jnp.where(j == n_blk - 1, 2, 1))
        ws = jnp.clip(NA_Q_ROWS * j - NA_WIN_H // 2, 0, grid_rows - NA_K_ROWS)
        rows = pl.ds(sub * tq, tq)
        qv = q_ref[rows, :]
        m_run = acc = l_part = None
        for c0 in [None] + list(range(0, n_keys, key_chunk)):
            if c0 is None:
                s = lax.dot_general(qv, kc_ref[...], nt, preferred_element_type=F32)
                vv = vc_ref[...]
            else:
                krows = pl.ds(pl.multiple_of(ws * GRID_W + c0, GRID_W), key_chunk)
                s = (lax.dot_general(qv, k_ref[krows, :], nt, preferred_element_type=F32)
                     + tab_ref[case, :, c0:c0 + key_chunk])
                vv = v_ref[krows, :]
            m_new = jnp.max(s, axis=-1, keepdims=True)
            if m_run is not None:
                m_new = jnp.maximum(m_run, m_new)
            p = jnp.exp2(s - m_new)
            l_new = p[:, 0:LANES]
            for c in range(LANES, s.shape[1], LANES):
                l_new = l_new + p[:, c:c + LANES]
            pv = jnp.dot(p.astype(BF16), vv, preferred_element_type=F32)
            if m_run is None:
                acc, l_part = pv, l_new
            else:
                alpha = jnp.exp2(m_run - m_new)
                acc = alpha * acc + pv
                l_part = alpha * l_part + l_new
            m_run = m_new
        o_ref[rows, :] = (acc / jnp.sum(l_part, axis=-1, keepdims=True)).astype(o_ref.dtype)


def _attn_small_kernel(q_ref, k_ref, v_ref, _, o_ref):
    s = lax.dot_general(q_ref[...], k_ref[...], (((1,), (1,)), ((), ())), preferred_element_type=F32)
    p = jnp.exp2(s - jnp.max(s, axis=-1, keepdims=True))
    o = jnp.dot(p.astype(BF16), v_ref[...], preferred_element_type=F32)
    o_ref[...] = (o / jnp.sum(p, axis=-1, keepdims=True)).astype(o_ref.dtype)


def _na_pair_blocks(rpb):
    qc = np.arange(GRID_W)[:, None]
    kc = np.arange(GRID_W)[None, :]
    c0 = np.clip(qc - NA_WIN_W // 2, 0, GRID_W - NA_WIN_W)
    col_ok = (kc >= c0) & (kc < c0 + NA_WIN_W)
    dc = np.clip(kc - qc + NA_WIN_W - 1, 0, 2 * NA_WIN_W - 2)
    a = jnp.take(rpb.astype(F32), jnp.asarray(dc), axis=2)
    a = jnp.where(jnp.asarray(col_ok), a, MASK_NEG)
    neg = jnp.full_like(a, MASK_NEG)
    return jnp.concatenate([jnp.concatenate([a[:, :-1], a[:, 1:]], axis=-1),
                            jnp.concatenate([a, neg], axis=-1),
                            jnp.concatenate([neg, a], axis=-1)], axis=1)


def _na_table_plan(grid_rows):
    n_blk = grid_rows // NA_Q_ROWS
    n_dr = 2 * NA_WIN_H - 1

    def block_plan(j):
        ws = int(np.clip(NA_Q_ROWS * j - NA_WIN_H // 2, 0, grid_rows - NA_K_ROWS))
        rows = []
        for qa in range(NA_Q_ROWS):
            qr = NA_Q_ROWS * j + qa
            r0 = int(np.clip(qr - NA_WIN_H // 2, 0, grid_rows - NA_WIN_H))
            pairs = []
            for i in range(NA_K_ROWS // 2):
                kr0 = ws + 2 * i
                ok0 = r0 <= kr0 < r0 + NA_WIN_H
                ok1 = r0 <= kr0 + 1 < r0 + NA_WIN_H
                dr0 = kr0 - qr + NA_WIN_H - 1
                if ok0 and ok1:
                    pairs.append(dr0)
                elif ok0:
                    pairs.append(n_dr - 1 + dr0)
                elif ok1:
                    pairs.append(2 * n_dr - 1 + dr0 + 1)
                else:
                    pairs.append(-1)
            rows.append(tuple(pairs))
        return tuple(rows)

    plans = [block_plan(j) for j in range(n_blk)]
    cases = (plans[0], plans[min(1, n_blk - 1)], plans[n_blk - 1])
    for j in range(1, n_blk - 1):
        if plans[j] != cases[1]:
            raise ValueError("neighbourhood-attention middle blocks are not translation invariant")
    return cases


def _na_attention(qkv, rpb, batch, seq, ctx_len, d_model, ctx_out):
    m = qkv.shape[0]
    heads = NA_HEADS
    hd = d_model // heads
    grid_rows = seq // GRID_W
    n_blk = grid_rows // NA_Q_ROWS
    bps = NA_BLOCKS_PER_STEP if n_blk % NA_BLOCKS_PER_STEP == 0 else 1
    n_steps = n_blk // bps
    tq = bps * NA_Q_ROWS * GRID_W
    tk = NA_K_ROWS * GRID_W
    cb = (batch * seq) // ctx_len
    k0, v0 = d_model // hd, 2 * d_model // hd
    pairs = _na_pair_blocks(rpb)
    n_pairs = pairs.shape[1]
    out_shape = jax.ShapeDtypeStruct((m, d_model), BF16)

    lat = pl.pallas_call(
        functools.partial(_na_kernel, grid_rows=grid_rows, n_blk=n_blk, blocks_per_step=bps,
                          key_chunk=min(tk, NA_KEY_CHUNK), plan=_na_table_plan(grid_rows)),
        grid=(heads, batch, n_steps),
        in_specs=[pl.BlockSpec((tq, hd), lambda h, b, j: (b * n_steps + j, h)),
                  pl.BlockSpec((seq, hd), lambda h, b, j: (b, k0 + h)),
                  pl.BlockSpec((seq, hd), lambda h, b, j: (b, v0 + h)),
                  pl.BlockSpec((ctx_len, hd), lambda h, b, j: (cb + b, k0 + h)),
                  pl.BlockSpec((ctx_len, hd), lambda h, b, j: (cb + b, v0 + h)),
                  pl.BlockSpec((1, n_pairs, GRID_W, 2 * GRID_W), lambda h, b, j: (h, 0, 0, 0)),
                  pl.BlockSpec(memory_space=pl.ANY)],
        out_specs=pl.BlockSpec((tq, hd), lambda h, b, j: (b * n_steps + j, h)),
        out_shape=out_shape,
        scratch_shapes=[pltpu.VMEM((3, NA_Q_ROWS * GRID_W, tk), F32)],
        input_output_aliases={6: 0},
        compiler_params=_cp("arbitrary", "arbitrary", "arbitrary"),
        name="na_attn_latent",
    )(qkv, qkv, qkv, qkv, qkv, pairs, jnp.zeros(out_shape.shape, BF16))
    if not ctx_out:
        return lat
    return pl.pallas_call(
        _attn_small_kernel,
        grid=(batch, heads),
        in_specs=[pl.BlockSpec((ctx_len, hd), lambda b, h: (cb + b, h)),
                  pl.BlockSpec((ctx_len, hd), lambda b, h: (cb + b, k0 + h)),
                  pl.BlockSpec((ctx_len, hd), lambda b, h: (cb + b, v0 + h)),
                  pl.BlockSpec(memory_space=pl.ANY)],
        out_specs=pl.BlockSpec((ctx_len, hd), lambda b, h: (cb + b, h)),
        out_shape=out_shape,
        input_output_aliases={3: 0},
        compiler_params=_cp("parallel", "parallel"),
        name="na_attn_ctx",
    )(qkv, qkv, qkv, lat)


def _rope_tables(seq, pad_rows):
    pos = np.arange(seq)
    n_freq = MLA_ROPE // 4
    inv_freq = jnp.power(ROPE_THETA, -jnp.arange(n_freq, dtype=F32) / n_freq)
    rows = jnp.asarray(pos // GRID_W, F32)[:, None] * inv_freq
    cols = jnp.asarray(pos % GRID_W, F32)[:, None] * inv_freq
    ang = jnp.concatenate([rows, rows, cols, cols], axis=-1)
    sign = np.tile(np.concatenate([-np.ones(n_freq), np.ones(n_freq)]), 2).astype(np.float32)
    cos_t = jnp.concatenate([jnp.cos(ang), jnp.zeros((seq, LANES - MLA_ROPE), F32)], axis=1)
    sin_t = jnp.concatenate([jnp.sin(ang) * sign, jnp.zeros((seq, LANES - MLA_ROPE), F32)], axis=1)
    ident = jnp.concatenate([jnp.ones((pad_rows, MLA_ROPE), F32), jnp.zeros((pad_rows, LANES - MLA_ROPE), F32)], 1)
    return (jnp.concatenate([cos_t, ident], axis=0),
            jnp.concatenate([sin_t, jnp.zeros((pad_rows, LANES), F32)], axis=0))


def _rope_partner_perm():
    n_freq = MLA_ROPE // 4
    dd = np.arange(MLA_ROPE)
    return np.where(dd % (2 * n_freq) < n_freq, dd + n_freq, dd - n_freq)


def kernel(x, c, ctx, c_ctx, ada_w1, ada_w2, ada_b, norm_mix, norm_ffn, ffn_w_gate, ffn_w_up, ffn_w_down,
           hyb_w_in, ssd_conv_w, ssd_conv_b, ssd_dt_bias, ssd_a_log, ssd_d, ssd_norm,
           mla_q_norm, mla_w_q_up, mla_kv_norm, mla_w_kv_up, hyb_w_out,
           na_w_qkv, na_rpb, na_w_out, final_norm):
    batch, seq, d_model = x.shape
    ctx_len = ctx.shape[1]
    depth = ada_w1.shape[0]
    n_lat = batch * seq
    m = n_lat + batch * ctx_len
    d_inner = ssd_norm.shape[-1]
    heads = d_inner // SSD_HEAD_DIM
    conv_dim = ssd_conv_w.shape[1]
    q_rank = mla_q_norm.shape[-1]
    kv_rank = mla_kv_norm.shape[-1]
    tseg = int(np.gcd(seq, batch * ctx_len))
    tseg = min(tseg, 1024)
    tm_big = tseg
    tm_half = min(tseg, 512)
    tr = min(tseg, 256)

    def seg(row0):
        return jnp.where(row0 < n_lat, row0 // seq, batch)

    h = jnp.concatenate([x.reshape(n_lat, d_model), ctx.reshape(batch * ctx_len, d_model)], axis=0)
    cond = jnp.zeros((COND_ROWS, d_model), F32).at[:batch].set(c).at[batch].set(c_ctx)
    mods_all = _ada(cond, ada_w1, ada_w2, ada_b).reshape(depth, COND_ROWS * N_MOD, 1, d_model)

    dt_cols = _round_up(2 * heads, LANES)
    perm = _rope_partner_perm()
    cos_t, sin_t = _rope_tables(seq, tm_big)
    lat_tiles = n_lat // tm_big
    tiles_per_seq = seq // tm_big

    def rope_table_idx(i):
        return jnp.where(i < lat_tiles, i % tiles_per_seq, tiles_per_seq)

    for layer in range(depth):
        ctx_out = layer < depth - 1
        rows = m if ctx_out else n_lat
        mods = mods_all[layer]
        u = _norm_rows(h, d_model, 0, norm_mix[layer], BF16, m, tr, mods, (0, 1), seg)
        if layer % 2 == 0:
            e = layer // 2
            w_in = hyb_w_in[e]
            s0 = d_inner + conv_dim
            s1 = s0 + 2 * heads
            s2 = s1 + q_rank
            s3 = s2 + kv_rank
            w_kpe = w_in[:, s3:]
            w_p2 = jnp.concatenate([w_in[:, s1:s2], w_in[:, s2:s3],
                                    jnp.pad(w_in[:, s0:s1], ((0, 0), (0, dt_cols - 2 * heads))),
                                    w_kpe, w_kpe[:, perm]], axis=1).astype(BF16)
            kv_idx = q_rank // kv_rank
            dt_idx = (q_rank + kv_rank) // dt_cols
            kpe_idx = (q_rank + kv_rank + dt_cols) // LANES
            p1 = _mm(u, hyb_w_in, BF16, tm_big, _tile(s0, 512), layer=e, n=s0)
            p2 = _mm(u, w_p2, F32, tm_big, _tile(w_p2.shape[1], 1024))
            xbc = _conv_silu(p1, d_inner, conv_dim, ssd_conv_w[e], ssd_conv_b[e], n_lat, seq, ctx_len,
                             min(ctx_len, 256), _tile(int(np.gcd(conv_dim, d_inner)), 2048))
            y2 = _ssd_scan(xbc, p2, dt_idx, ssd_dt_bias[e], ssd_a_log[e], batch, seq, ctx_len, d_inner)
            ssd_out = _ssd_finish(y2, xbc, p1, ssd_d[e], ssd_norm[e], d_inner, tr)

            qn = _norm_rows(p2, q_rank, 0, mla_q_norm[e], BF16, m, tr)
            kvn = _norm_rows(p2, kv_rank, kv_idx, mla_kv_norm[e], BF16, m, tr)
            wq = mla_w_q_up[e].reshape(q_rank, MLA_HEADS, MLA_NOPE + MLA_ROPE)
            wq_rope = wq[:, :, MLA_NOPE:]
            wq3 = jnp.concatenate([wq[:, :, :MLA_NOPE], wq_rope, wq_rope[:, :, perm]], axis=2)
            wq3 = wq3.reshape(q_rank, MLA_HEADS * 2 * LANES).astype(BF16)
            qh = _mm_qrope(qn, wq3, cos_t, sin_t, tm_big, min(4, MLA_HEADS), rope_table_idx)
            kv = _mm(kvn, mla_w_kv_up[e].astype(BF16), BF16, tm_big, _tile(MLA_HEADS * 2 * LANES, 2048))
            attn = _mla_attention(qh, kv, p2, kpe_idx, cos_t, sin_t, batch, seq, ctx_len, min(seq, MLA_Q_TILE))
            w_out = hyb_w_out[e].astype(BF16)
            h = _mm_res([ssd_out, attn], w_out, h, mods, 2, seg, rows, tm_half, _tile(d_model, 512))
        else:
            o = layer // 2
            qkv = _mm(u, na_w_qkv, BF16, tm_big, _tile(d_model, 512), layer=o,
                      scaled_cols=d_model, out_scale=float((d_model // NA_HEADS) ** -0.5 * LOG2_E))
            att = _na_attention(qkv, na_rpb[o], batch, seq, ctx_len, d_model, ctx_out)
            h = _mm_res([att], na_w_out[o].astype(BF16), h, mods, 2, seg, rows, tm_big, _tile(d_model, 512))
        u2 = _norm_rows(h, d_model, 0, norm_ffn[layer], BF16, rows, tr, mods, (3, 4), seg)
        hidden = _mm_swiglu(u2, ffn_w_gate, ffn_w_up, layer, rows, tm_big, _tile(ffn_w_gate.shape[-1], 256))
        h = _mm_res([hidden], ffn_w_down[layer].astype(BF16), h, mods, 5, seg, rows, tm_half,
                    _tile(d_model, 256))
    out = _norm_rows(h, d_model, 0, final_norm, F32, n_lat, tr)
    return out.reshape(batch, seq, d_model)
```

```python
import functools

import numpy as np
import jax
import jax.numpy as jnp
from jax import lax
from jax.experimental import pallas as pl
from jax.experimental.pallas import tpu as pltpu

F32 = jnp.float32
BF16 = jnp.bfloat16

NORM_EPS = 1e-6
N_MOD = 6
GRID_W = 64
ROPE_THETA = 10000.0
SSD_HEAD_DIM = 64
SSD_GROUPS = 8
SSD_STATE = 128
SSD_CONV = 5
SSD_CHUNK = 128
MLA_HEADS = 32
MLA_NOPE = 128
MLA_ROPE = 64
MLA_V = 128
NA_HEADS = 32
NA_WIN_H = 8
NA_WIN_W = 16
NA_Q_ROWS = 8
NA_K_ROWS = 16
NA_BLOCKS_PER_STEP = 2
NA_KEY_CHUNK = 256
MASK_NEG = -1e30
LOG2_E = 1.4426950408889634
MLA_KEY_CHUNK = 256
MLA_Q_TILE = 1024
LANES = 128
BF16_SUBLANES = 16
VMEM_LIMIT_BYTES = 56 * 2 ** 20
COND_ROWS = 16
WIDE_ROW_TILE = 2304


def _cp(*sem):
    return pltpu.CompilerParams(dimension_semantics=sem, vmem_limit_bytes=VMEM_LIMIT_BYTES)


def _tile(n, pref):
    if n <= pref:
        return n
    t = (pref // LANES) * LANES
    while t >= LANES:
        if n % t == 0:
            return t
        t -= LANES
    raise ValueError(f"no lane-aligned tile for {n}")


def _silu(v):
    return v * jax.nn.sigmoid(v)


def _ada_kernel(c_ref, w1_ref, w2_ref, b_ref, o_ref):
    cnd = c_ref[...]
    t = jnp.dot(_silu(cnd).astype(BF16), w1_ref[0].astype(BF16), preferred_element_type=F32)
    o_ref[0] = jnp.dot(t.astype(BF16), w2_ref[0].astype(BF16), preferred_element_type=F32) + b_ref[0]


def _ada(cond, w1, w2, bias):
    depth, d, r = w1.shape
    n = w2.shape[-1]
    tn = _tile(n, 2048)
    return pl.pallas_call(
        _ada_kernel,
        grid=(depth, n // tn),
        in_specs=[pl.BlockSpec((COND_ROWS, d), lambda l, j: (0, 0)),
                  pl.BlockSpec((1, d, r), lambda l, j: (l, 0, 0)),
                  pl.BlockSpec((1, r, tn), lambda l, j: (l, 0, j)),
                  pl.BlockSpec((1, 1, tn), lambda l, j: (l, 0, j))],
        out_specs=pl.BlockSpec((1, COND_ROWS, tn), lambda l, j: (l, 0, j)),
        out_shape=jax.ShapeDtypeStruct((depth, COND_ROWS, n), F32),
        compiler_params=_cp("arbitrary", "arbitrary"),
        name="ada",
    )(cond, w1, w2, bias.reshape(depth, 1, n))


def _norm_kernel(x_ref, g_ref, *rest, modulated):
    o_ref = rest[-1]
    xv = x_ref[...].astype(F32)
    y = xv * lax.rsqrt(jnp.mean(xv * xv, axis=-1, keepdims=True) + NORM_EPS) * g_ref[...]
    if modulated:
        sh_ref, sc_ref = rest[0], rest[1]
        y = y * (1.0 + sc_ref[0]) + sh_ref[0]
    o_ref[...] = y.astype(o_ref.dtype)


def _norm_rows(xa, width, col_idx, gain, out_dtype, n_rows, tr, mods=None, which=None, seg=None):
    in_specs = [pl.BlockSpec((tr, width), lambda i: (i, col_idx)),
                pl.BlockSpec((1, width), lambda i: (0, 0))]
    args = [xa, gain.reshape(1, width).astype(F32)]
    if mods is not None:
        sh_w, sc_w = which
        in_specs += [pl.BlockSpec((1, 1, width), lambda i: (seg(i * tr) * N_MOD + sh_w, 0, 0)),
                     pl.BlockSpec((1, 1, width), lambda i: (seg(i * tr) * N_MOD + sc_w, 0, 0))]
        args += [mods, mods]
    return pl.pallas_call(
        functools.partial(_norm_kernel, modulated=mods is not None),
        grid=(n_rows // tr,),
        in_specs=in_specs,
        out_specs=pl.BlockSpec((tr, width), lambda i: (i, 0)),
        out_shape=jax.ShapeDtypeStruct((n_rows, width), out_dtype),
        compiler_params=_cp("parallel"),
        name="norm",
    )(*args)


def _x_spec(tm, k, single_buffer):
    if single_buffer:
        return pl.BlockSpec((tm, k), lambda i, j: (i, 0), pipeline_mode=pl.Buffered(1))
    return pl.BlockSpec((tm, k), lambda i, j: (i, 0))


def _row_tile(rows, pref):
    t = (min(pref, rows) // BF16_SUBLANES) * BF16_SUBLANES
    while rows % t:
        t -= BF16_SUBLANES
    return t


def _w_spec(w, layer, k, tn):
    if w.ndim == 3:
        return pl.BlockSpec((None, k, tn), lambda i, j: (layer, 0, j))
    return pl.BlockSpec((k, tn), lambda i, j: (0, j))


def _mm_kernel(x_ref, w_ref, o_ref, *, scaled_tiles, out_scale):
    acc = jnp.dot(x_ref[...], w_ref[...].astype(BF16), preferred_element_type=F32)
    if scaled_tiles:
        acc = acc * jnp.where(pl.program_id(1) < scaled_tiles, out_scale, 1.0)
    o_ref[...] = acc.astype(o_ref.dtype)


def _mm(xa, w, out_dtype, tm, tn, layer=None, n=None, scaled_cols=0, out_scale=1.0, single_buffer_x=False):
    m, k = xa.shape
    n = w.shape[-1] if n is None else n
    return pl.pallas_call(
        functools.partial(_mm_kernel, scaled_tiles=scaled_cols // tn, out_scale=out_scale),
        grid=(m // tm, n // tn),
        in_specs=[_x_spec(tm, k, single_buffer_x), _w_spec(w, layer, k, tn)],
        out_specs=pl.BlockSpec((tm, tn), lambda i, j: (i, j)),
        out_shape=jax.ShapeDtypeStruct((m, n), out_dtype),
        compiler_params=_cp("parallel", "arbitrary"),
        name="mm",
    )(xa, w)


def _mm_swiglu_kernel(x_ref, wg_ref, wu_ref, o_ref):
    xv = x_ref[...]
    gate = jnp.dot(xv, wg_ref[...].astype(BF16), preferred_element_type=F32)
    up = jnp.dot(xv, wu_ref[...].astype(BF16), preferred_element_type=F32)
    o_ref[...] = (_silu(gate) * up).astype(o_ref.dtype)


def _mm_swiglu(xa, wg, wu, layer, rows, tm, tn):
    m, k = xa.shape
    n = wg.shape[-1]
    return pl.pallas_call(
        _mm_swiglu_kernel,
        grid=(rows // tm, n // tn),
        in_specs=[_x_spec(tm, k, True), _w_spec(wg, layer, k, tn), _w_spec(wu, layer, k, tn)],
        out_specs=pl.BlockSpec((tm, tn), lambda i, j: (i, j)),
        out_shape=jax.ShapeDtypeStruct((m, n), BF16),
        compiler_params=_cp("parallel", "arbitrary"),
        name="mm_swiglu",
    )(xa, wg, wu)


def _mm_res_kernel(*refs, n_pairs):
    o_ref = refs[-1]
    res_ref, gate_ref = refs[2 * n_pairs], refs[2 * n_pairs + 1]
    acc = None
    for i in range(n_pairs):
        part = jnp.dot(refs[i][...], refs[n_pairs + i][...], preferred_element_type=F32)
        acc = part if acc is None else acc + part
    o_ref[...] = res_ref[...] + gate_ref[0] * acc


def _mm_res(xas, w, res, mods, which, seg, rows, tm, tn, single_buffer_x=False):
    m, n = res.shape
    n_pairs = len(xas)
    k = xas[0].shape[1]
    in_specs = [_x_spec(tm, k, single_buffer_x) for _ in xas]
    in_specs += [pl.BlockSpec((k, tn), functools.partial(lambda i, j, p: (p, j), p=p)) for p in range(n_pairs)]
    in_specs += [pl.BlockSpec((tm, tn), lambda i, j: (i, j)),
                 pl.BlockSpec((1, 1, tn), lambda i, j: (seg(i * tm) * N_MOD + which, 0, j))]
    return pl.pallas_call(
        functools.partial(_mm_res_kernel, n_pairs=n_pairs),
        grid=(rows // tm, n // tn),
        in_specs=in_specs,
        out_specs=pl.BlockSpec((tm, tn), lambda i, j: (i, j)),
        out_shape=jax.ShapeDtypeStruct((m, n), F32),
        compiler_params=_cp("parallel", "arbitrary"),
        name="mm_res",
    )(*xas, *([w] * n_pairs), res, mods)


def _mm_qrope_kernel(x_ref, w_ref, cos_ref, sin_ref, o_ref, *, heads, out_scale):
    acc = jnp.dot(x_ref[...], w_ref[...], preferred_element_type=F32) * out_scale
    cs, sn = cos_ref[...], sin_ref[...]
    for hh in range(heads):
        base = hh * 2 * LANES
        o_ref[:, base:base + LANES] = acc[:, base:base + LANES].astype(o_ref.dtype)
        half = acc[:, base + LANES:base + 2 * LANES]
        rot = half * cs + pltpu.roll(half, MLA_ROPE, 1) * sn
        o_ref[:, base + LANES:base + 2 * LANES] = rot.astype(o_ref.dtype)


def _mm_qrope(xa, w, cos_t, sin_t, tm, heads_per_step, table_idx):
    m, k = xa.shape
    n = w.shape[1]
    tn = heads_per_step * 2 * LANES
    out_scale = float((MLA_NOPE + MLA_ROPE) ** -0.5 * LOG2_E)
    return pl.pallas_call(
        functools.partial(_mm_qrope_kernel, heads=heads_per_step, out_scale=out_scale),
        grid=(m // tm, n // tn),
        in_specs=[pl.BlockSpec((tm, k), lambda i, j: (i, 0)),
                  pl.BlockSpec((k, tn), lambda i, j: (0, j)),
                  pl.BlockSpec((tm, LANES), lambda i, j: (table_idx(i), 0)),
                  pl.BlockSpec((tm, LANES), lambda i, j: (table_idx(i), 0))],
        out_specs=pl.BlockSpec((tm, tn), lambda i, j: (i, j)),
        out_shape=jax.ShapeDtypeStruct((m, n), BF16),
        compiler_params=_cp("parallel", "arbitrary"),
        name="mm_qrope",
    )(xa, w, cos_t, sin_t)


def _conv_kernel(prev_ref, cur_ref, next_ref, w_ref, b_ref, o_ref, *, n_lat_blocks, blocks_per_seq, blocks_per_ctx):
    p = pl.program_id(0)
    is_ctx = p >= n_lat_blocks
    q = jnp.where(is_ctx, (p - n_lat_blocks) % blocks_per_ctx, p % blocks_per_seq)
    q_last = jnp.where(is_ctx, blocks_per_ctx - 1, blocks_per_seq - 1)
    keep_prev = jnp.where(q == 0, 0.0, 1.0)
    keep_next = jnp.where(q == q_last, 0.0, 1.0)
    xc = cur_ref[...].astype(F32)
    rows = xc.shape[0]
    halo = prev_ref.shape[0]
    full = jnp.concatenate([prev_ref[...].astype(F32) * keep_prev, xc,
                            next_ref[...].astype(F32) * keep_next], axis=0)
    total = rows + 2 * halo
    wv = w_ref[...]
    acc = jnp.zeros_like(xc) + b_ref[...]
    for tap in range(SSD_CONV):
        shift = (SSD_CONV // 2 - tap) % total
        moved = full if shift == 0 else pltpu.roll(full, shift, 0)
        acc = acc + wv[tap:tap + 1, :] * moved[halo:halo + rows]
    o_ref[...] = _silu(acc).astype(o_ref.dtype)


def _conv_silu(p1, col0, width, conv_w, conv_b, n_lat_rows, seq, ctx_len, rb, tc):
    m = p1.shape[0]
    halo = BF16_SUBLANES
    cb0 = col0 // tc
    per = rb // halo
    last_halo = m // halo - 1
    return pl.pallas_call(
        functools.partial(_conv_kernel, n_lat_blocks=n_lat_rows // rb, blocks_per_seq=seq // rb,
                          blocks_per_ctx=ctx_len // rb),
        grid=(m // rb, width // tc),
        in_specs=[pl.BlockSpec((halo, tc), lambda p, j: (jnp.maximum(p * per - 1, 0), cb0 + j)),
                  pl.BlockSpec((rb, tc), lambda p, j: (p, cb0 + j)),
                  pl.BlockSpec((halo, tc), lambda p, j: (jnp.minimum((p + 1) * per, last_halo), cb0 + j)),
                  pl.BlockSpec((SSD_CONV, tc), lambda p, j: (0, j)),
                  pl.BlockSpec((1, tc), lambda p, j: (0, j))],
        out_specs=pl.BlockSpec((rb, tc), lambda p, j: (p, j)),
        out_shape=jax.ShapeDtypeStruct((m, width), BF16),
        compiler_params=_cp("parallel", "arbitrary"),
        name="conv_silu",
    )(p1, p1, p1, jnp.transpose(conv_w).astype(F32), conv_b.reshape(1, width).astype(F32))


def _split3(v):
    hi = v.astype(BF16)
    r1 = v - hi.astype(F32)
    mid = r1.astype(BF16)
    lo = (r1 - mid.astype(F32)).astype(BF16)
    return hi, mid, lo


def _dot_exact_rhs(sel, v):
    return sum(jnp.dot(sel, part, preferred_element_type=F32) for part in _split3(v))


def _dot_exact_lhs(v, sel):
    return sum(jnp.dot(part, sel, preferred_element_type=F32) for part in _split3(v))


def _ssd_kernel(xs_ref, b_ref, c_ref, dt_ref, bias_ref, alog_ref, y_ref, st_ref, ex_ref, *, heads,
                heads_per_group):
    d = pl.program_id(1)
    t = xs_ref.shape[0]
    hd = SSD_HEAD_DIM
    n = SSD_STATE
    width = heads_per_group * hd
    ncol = dt_ref.shape[1]

    @pl.when(pl.program_id(2) == 0)
    def _():
        st_ref[...] = jnp.zeros_like(st_ref)
        ri = lax.broadcasted_iota(jnp.int32, ex_ref.shape, 0)
        ci = lax.broadcasted_iota(jnp.int32, ex_ref.shape, 1)
        ex_ref[...] = (ri == d * heads + ci // hd).astype(BF16)

    xv = dt_ref[...] + bias_ref[...]
    dtv = jnp.maximum(xv, 0.0) + jnp.log1p(jnp.exp(-jnp.abs(xv)))
    da = dtv * (-jnp.exp(alog_ref[...]))
    qi = lax.broadcasted_iota(jnp.int32, (t, t), 0)
    si = lax.broadcasted_iota(jnp.int32, (t, t), 1)
    mask = (si - qi) * jnp.where(d == 0, 1, -1) <= 0
    acum = _dot_exact_rhs(mask.astype(BF16), da)

    expand = ex_ref[...]
    acum_x = _dot_exact_lhs(acum, expand)
    dtv_x = _dot_exact_lhs(dtv, expand)
    acum_t_all = jnp.transpose(acum)
    acum_t = jnp.where(d == 0, acum_t_all[0:heads], acum_t_all[heads:2 * heads])
    a_end = jnp.where(d == 0, acum_x[t - 1:t, :], acum_x[0:1, :])

    xdt = xs_ref[...].astype(F32) * dtv_x
    xdt_b = xdt.astype(BF16)
    scale_q = jnp.exp(acum_x)
    xw = (xdt * jnp.exp(a_end - acum_x)).astype(BF16)
    decay_end = jnp.exp(a_end)

    lane = lax.broadcasted_iota(jnp.int32, (t, LANES), 1)
    for g in range(SSD_GROUPS):
        gs = slice(g * width, (g + 1) * width)
        cg = c_ref[:, g * n:(g + 1) * n]
        bg = b_ref[:, g * n:(g + 1) * n]
        cb = lax.dot_general(cg, bg, (((1,), (1,)), ((), ())), preferred_element_type=F32)
        state = st_ref[g]
        y_state = jnp.dot(cg, state.astype(BF16), preferred_element_type=F32)
        bg_t = jnp.transpose(bg.astype(F32)).astype(BF16)
        st_ref[g] = state * decay_end[:, gs] + jnp.dot(bg_t, xw[:, gs], preferred_element_type=F32)
        for p in range(heads_per_group // 2):
            sl = slice(g * width + p * LANES, g * width + (p + 1) * LANES)
            mats = []
            for hh in (g * heads_per_group + 2 * p, g * heads_per_group + 2 * p + 1):
                a_q = acum_x[:, hh * hd:hh * hd + 1]
                a_s = acum_t[hh:hh + 1, :]
                mats.append(cb * jnp.exp(jnp.where(mask, a_q - a_s, -jnp.inf)))
            lcat = jnp.concatenate(mats, axis=1).astype(BF16)
            xp = xdt_b[:, sl]
            xcat = jnp.concatenate([jnp.where(lane < hd, xp, jnp.zeros_like(xp)),
                                    jnp.where(lane >= hd, xp, jnp.zeros_like(xp))], axis=0)
            y_in = jnp.dot(lcat, xcat, preferred_element_type=F32)
            y_ref[0, :, sl] = (y_in + y_state[:, p * LANES:(p + 1) * LANES] * scale_q[:, sl]).astype(y_ref.dtype)


def _ssd_scan(xbc, p2, dt_col_idx, dt_bias, a_log, batch, seq, ctx_len, d_inner):
    m = xbc.shape[0]
    t = SSD_CHUNK
    heads = d_inner // SSD_HEAD_DIM
    hpg = heads // SSD_GROUPS
    width = hpg * SSD_HEAD_DIM
    n = SSD_STATE
    nc_ctx, nc_lat = ctx_len // t, seq // t
    nch = nc_ctx + nc_lat
    ctx_base = batch * nc_lat
    ncol = _round_up(2 * heads, LANES)

    def row_block(b, d, i):
        ctx_blk = ctx_base + b * nc_ctx + jnp.where(d == 0, i, nc_ctx - 1 - i)
        lat_blk = b * nc_lat + jnp.where(d == 0, i - nc_ctx, nch - 1 - i)
        return jnp.where(i < nc_ctx, ctx_blk, lat_blk)

    gn = SSD_GROUPS * n
    b_col = d_inner // gn
    pad = ncol - 2 * heads
    bias = jnp.pad(dt_bias.reshape(1, 2 * heads).astype(F32), ((0, 0), (0, pad)))
    alog = jnp.pad(a_log.reshape(1, 2 * heads).astype(F32), ((0, 0), (0, pad)))
    return pl.pallas_call(
        functools.partial(_ssd_kernel, heads=heads, heads_per_group=hpg),
        grid=(batch, 2, nch),
        in_specs=[pl.BlockSpec((t, d_inner), lambda b, d, i: (row_block(b, d, i), 0)),
                  pl.BlockSpec((t, gn), lambda b, d, i: (row_block(b, d, i), b_col)),
                  pl.BlockSpec((t, gn), lambda b, d, i: (row_block(b, d, i), b_col + 1)),
                  pl.BlockSpec((t, ncol), lambda b, d, i: (row_block(b, d, i), dt_col_idx)),
                  pl.BlockSpec((1, ncol), lambda b, d, i: (0, 0)),
                  pl.BlockSpec((1, ncol), lambda b, d, i: (0, 0))],
        out_specs=pl.BlockSpec((1, t, d_inner), lambda b, d, i: (d, row_block(b, d, i), 0)),
        out_shape=jax.ShapeDtypeStruct((2, m, d_inner), BF16),
        scratch_shapes=[pltpu.VMEM((SSD_GROUPS, n, width), F32), pltpu.VMEM((ncol, d_inner), BF16)],
        compiler_params=_cp("parallel", "parallel", "arbitrary"),
        name="ssd_scan",
    )(xbc, xbc, xbc, p2, bias, alog)


def _round_up(v, mult):
    return -(-v // mult) * mult


def _ssd_finish_kernel(y_ref, xs_ref, z_ref, d_ref, g_ref, o_ref):
    y = y_ref[0].astype(F32) + y_ref[1].astype(F32) + d_ref[...] * xs_ref[...].astype(F32)
    zv = z_ref[...].astype(F32)
    y = y * _silu(zv)
    gw = y.shape[1] // SSD_GROUPS
    for gi in range(SSD_GROUPS):
        sl = slice(gi * gw, (gi + 1) * gw)
        yg = y[:, sl]
        inv = lax.rsqrt(jnp.mean(yg * yg, axis=-1, keepdims=True) + NORM_EPS)
        o_ref[:, sl] = (yg * inv * g_ref[:, sl]).astype(o_ref.dtype)


def _ssd_finish(y2, xbc, p1, d_skip, ssd_norm, d_inner, tr):
    m = xbc.shape[0]
    d_x = jnp.repeat(d_skip.astype(F32), SSD_HEAD_DIM).reshape(1, d_inner)
    return pl.pallas_call(
        _ssd_finish_kernel,
        grid=(m // tr,),
        in_specs=[pl.BlockSpec((2, tr, d_inner), lambda i: (0, i, 0)),
                  pl.BlockSpec((tr, d_inner), lambda i: (i, 0)),
                  pl.BlockSpec((tr, d_inner), lambda i: (i, 0)),
                  pl.BlockSpec((1, d_inner), lambda i: (0, 0)),
                  pl.BlockSpec((1, d_inner), lambda i: (0, 0))],
        out_specs=pl.BlockSpec((tr, d_inner), lambda i: (i, 0)),
        out_shape=jax.ShapeDtypeStruct((m, d_inner), BF16),
        compiler_params=_cp("parallel"),
        name="ssd_finish",
    )(y2, xbc, p1, d_x, ssd_norm.reshape(1, d_inner).astype(F32))


def _mla_attn_kernel(*refs, ctx_len, with_latent, key_chunk):
    if with_latent:
        q_ref, kc_ref, vc_ref, pc_ref, kl_ref, vl_ref, pl_ref, cos_ref, sin_ref, _, o_ref, kt_ref, vs_ref = refs
        seq = kl_ref.shape[0]
    else:
        q_ref, kc_ref, vc_ref, pc_ref, _, o_ref, kt_ref, vs_ref = refs
        seq = 0

    @pl.when(pl.program_id(2) == 0)
    def _():
        lane = lax.broadcasted_iota(jnp.int32, (ctx_len, LANES), 1)
        kt_ref[0:LANES, 0:ctx_len] = jnp.transpose(kc_ref[...].astype(F32)).astype(BF16)
        kt_ref[LANES:, 0:ctx_len] = jnp.transpose(jnp.where(lane < MLA_ROPE, pc_ref[...], 0.0)).astype(BF16)
        vs_ref[0:ctx_len, :] = vc_ref[...]
        for a in range(0, seq, key_chunk):
            rows = slice(a, a + key_chunk)
            cols = slice(ctx_len + a, ctx_len + a + key_chunk)
            kt_ref[0:LANES, cols] = jnp.transpose(kl_ref[rows, :].astype(F32)).astype(BF16)
            blk = pl_ref[rows, :]
            rot = blk * cos_ref[rows, :] + pltpu.roll(blk, MLA_ROPE, 1) * sin_ref[rows, :]
            kt_ref[LANES:, cols] = jnp.transpose(rot).astype(BF16)
            vs_ref[cols, :] = vl_ref[rows, :]

    qv = q_ref[...]
    bounds = [(0, ctx_len)] + [(ctx_len + a, ctx_len + a + key_chunk) for a in range(0, seq, key_chunk)]
    m_run = acc = l_part = None
    for lo, hi in bounds:
        s = jnp.dot(qv, kt_ref[:, lo:hi], preferred_element_type=F32)
        m_new = jnp.max(s, axis=-1, keepdims=True)
        if m_run is not None:
            m_new = jnp.maximum(m_run, m_new)
        p = jnp.exp2(s - m_new)
        l_new = p[:, 0:LANES]
        for c in range(LANES, hi - lo, LANES):
            l_new = l_new + p[:, c:c + LANES]
        pv = jnp.dot(p.astype(BF16), vs_ref[lo:hi, :], preferred_element_type=F32)
        if m_run is None:
            acc, l_part = pv, l_new
        else:
            alpha = jnp.exp2(m_run - m_new)
            acc = alpha * acc + pv
            l_part = alpha * l_part + l_new
        m_run = m_new
    o_ref[...] = (acc / jnp.sum(l_part, axis=-1, keepdims=True)).astype(o_ref.dtype)


def _mla_attention(qh, kv, p2, kpe_idx, cos_t, sin_t, batch, seq, ctx_len, tq):
    m = qh.shape[0]
    heads = MLA_HEADS
    key_chunk = min(seq, MLA_KEY_CHUNK)
    cb = (batch * seq) // ctx_len
    nq = seq // tq
    out_shape = jax.ShapeDtypeStruct((m, heads * MLA_V), BF16)
    ctx_specs = [pl.BlockSpec((ctx_len, LANES), lambda b, h, i: (cb + b, 2 * h)),
                 pl.BlockSpec((ctx_len, LANES), lambda b, h, i: (cb + b, 2 * h + 1)),
                 pl.BlockSpec((ctx_len, LANES), lambda b, h, i: (cb + b, kpe_idx))]
    lat = pl.pallas_call(
        functools.partial(_mla_attn_kernel, ctx_len=ctx_len, with_latent=True, key_chunk=key_chunk),
        grid=(batch, heads, nq),
        in_specs=[pl.BlockSpec((tq, 2 * LANES), lambda b, h, i: (b * nq + i, h))] + ctx_specs + [
            pl.BlockSpec((seq, LANES), lambda b, h, i: (b, 2 * h)),
            pl.BlockSpec((seq, LANES), lambda b, h, i: (b, 2 * h + 1)),
            pl.BlockSpec((seq, LANES), lambda b, h, i: (b, kpe_idx)),
            pl.BlockSpec((seq, LANES), lambda b, h, i: (0, 0)),
            pl.BlockSpec((seq, LANES), lambda b, h, i: (0, 0)),
            pl.BlockSpec(memory_space=pl.ANY)],
        out_specs=pl.BlockSpec((tq, MLA_V), lambda b, h, i: (b * nq + i, h)),
        out_shape=out_shape,
        scratch_shapes=[pltpu.VMEM((2 * LANES, ctx_len + seq), BF16),
                        pltpu.VMEM((ctx_len + seq, MLA_V), BF16)],
        input_output_aliases={9: 0},
        compiler_params=_cp("parallel", "parallel", "arbitrary"),
        name="mla_attn_latent",
    )(qh, kv, kv, p2, kv, kv, p2, cos_t, sin_t, jnp.zeros(out_shape.shape, BF16))
    return pl.pallas_call(
        functools.partial(_mla_attn_kernel, ctx_len=ctx_len, with_latent=False, key_chunk=key_chunk),
        grid=(batch, heads, 1),
        in_specs=[pl.BlockSpec((ctx_len, 2 * LANES), lambda b, h, i: (cb + b, h))] + ctx_specs + [
            pl.BlockSpec(memory_space=pl.ANY)],
        out_specs=pl.BlockSpec((ctx_len, MLA_V), lambda b, h, i: (cb + b, h)),
        out_shape=out_shape,
        scratch_shapes=[pltpu.VMEM((2 * LANES, ctx_len), BF16), pltpu.VMEM((ctx_len, MLA_V), BF16)],
        input_output_aliases={4: 0},
        compiler_params=_cp("parallel", "parallel", "arbitrary"),
        name="mla_attn_ctx",
    )(qh, kv, kv, p2, lat)


def _na_kernel(q_ref, k_ref, v_ref, kc_ref, vc_ref, pair_ref, _, o_ref, tab_ref, *, grid_rows, n_blk,
               blocks_per_step, key_chunk, plan):
    step = pl.program_id(2)

    @pl.when(jnp.logical_and(pl.program_id(1) == 0, step == 0))
    def _():
        for case, per_row in enumerate(plan):
            for qa, per_pair in enumerate(per_row):
                for i, src in enumerate(per_pair):
                    if src < 0:
                        blk = jnp.full((GRID_W, 2 * GRID_W), MASK_NEG, F32)
                    else:
                        blk = pair_ref[0, src] * LOG2_E
                    tab_ref[case, qa * GRID_W:(qa + 1) * GRID_W, i * 2 * GRID_W:(i + 1) * 2 * GRID_W] = blk

    nt = (((1,), (1,)), ((), ()))
    tq = NA_Q_ROWS * GRID_W
    n_keys = NA_K_ROWS * GRID_W
    for sub in range(blocks_per_step):
        j = step * blocks_per_step + sub
        case = jnp.where(j == 0, 0, jnp.where(j == n_blk - 1, 2, 1))
        ws = jnp.clip(NA_Q_ROWS * j - NA_WIN_H // 2, 0, grid_rows - NA_K_ROWS)
        rows = pl.ds(sub * tq, tq)
        qv = q_ref[rows, :]
        m_run = acc = l_part = None
        for c0 in [None] + list(range(0, n_keys, key_chunk)):
            if c0 is None:
                s = lax.dot_general(qv, kc_ref[...], nt, preferred_element_type=F32)
                vv = vc_ref[...]
            else:
                krows = pl.ds(pl.multiple_of(ws * GRID_W + c0, GRID_W), key_chunk)
                s = (lax.dot_general(qv, k_ref[krows, :], nt, preferred_element_type=F32)
                     + tab_ref[case, :, c0:c0 + key_chunk])
                vv = v_ref[krows, :]
            m_new = jnp.max(s, axis=-1, keepdims=True)
            if m_run is not None:
                m_new = jnp.maximum(m_run, m_new)
            p = jnp.exp2(s - m_new)
            l_new = p[:, 0:LANES]
            for c in range(LANES, s.shape[1], LANES):
                l_new = l_new + p[:, c:c + LANES]
            pv = jnp.dot(p.astype(BF16), vv, preferred_element_type=F32)
            if m_run is None:
                acc, l_part = pv, l_new
            else:
                alpha = jnp.exp2(m_run - m_new)
                acc = alpha * acc + pv
                l_part = alpha * l_part + l_new
            m_run = m_new
        o_ref[rows, :] = (acc / jnp.sum(l_part, axis=-1, keepdims=True)).astype(o_ref.dtype)


def _attn_small_kernel(q_ref, k_ref, v_ref, _, o_ref):
    s = lax.dot_general(q_ref[...], k_ref[...], (((1,), (1,)), ((), ())), preferred_element_type=F32)
    p = jnp.exp2(s - jnp.max(s, axis=-1, keepdims=True))
    o = jnp.dot(p.astype(BF16), v_ref[...], preferred_element_type=F32)
    o_ref[...] = (o / jnp.sum(p, axis=-1, keepdims=True)).astype(o_ref.dtype)


def _na_pair_blocks(rpb):
    qc = np.arange(GRID_W)[:, None]
    kc = np.arange(GRID_W)[None, :]
    c0 = np.clip(qc - NA_WIN_W // 2, 0, GRID_W - NA_WIN_W)
    col_ok = (kc >= c0) & (kc < c0 + NA_WIN_W)
    dc = np.clip(kc - qc + NA_WIN_W - 1, 0, 2 * NA_WIN_W - 2)
    a = jnp.take(rpb.astype(F32), jnp.asarray(dc), axis=2)
    a = jnp.where(jnp.asarray(col_ok), a, MASK_NEG)
    neg = jnp.full_like(a, MASK_NEG)
    return jnp.concatenate([jnp.concatenate([a[:, :-1], a[:, 1:]], axis=-1),
                            jnp.concatenate([a, neg], axis=-1),
                            jnp.concatenate([neg, a], axis=-1)], axis=1)


def _na_table_plan(grid_rows):
    n_blk = grid_rows // NA_Q_ROWS
    n_dr = 2 * NA_WIN_H - 1

    def block_plan(j):
        ws = int(np.clip(NA_Q_ROWS * j - NA_WIN_H // 2, 0, grid_rows - NA_K_ROWS))
        rows = []
        for qa in range(NA_Q_ROWS):
            qr = NA_Q_ROWS * j + qa
            r0 = int(np.clip(qr - NA_WIN_H // 2, 0, grid_rows - NA_WIN_H))
            pairs = []
            for i in range(NA_K_ROWS // 2):
                kr0 = ws + 2 * i
                ok0 = r0 <= kr0 < r0 + NA_WIN_H
                ok1 = r0 <= kr0 + 1 < r0 + NA_WIN_H
                dr0 = kr0 - qr + NA_WIN_H - 1
                if ok0 and ok1:
                    pairs.append(dr0)
                elif ok0:
                    pairs.append(n_dr - 1 + dr0)
                elif ok1:
                    pairs.append(2 * n_dr - 1 + dr0 + 1)
                else:
                    pairs.append(-1)
            rows.append(tuple(pairs))
        return tuple(rows)

    plans = [block_plan(j) for j in range(n_blk)]
    cases = (plans[0], plans[min(1, n_blk - 1)], plans[n_blk - 1])
    for j in range(1, n_blk - 1):
        if plans[j] != cases[1]:
            raise ValueError("neighbourhood-attention middle blocks are not translation invariant")
    return cases


def _na_attention(qkv, rpb, batch, seq, ctx_len, d_model, ctx_out):
    m = qkv.shape[0]
    heads = NA_HEADS
    hd = d_model // heads
    grid_rows = seq // GRID_W
    n_blk = grid_rows // NA_Q_ROWS
    bps = NA_BLOCKS_PER_STEP if n_blk % NA_BLOCKS_PER_STEP == 0 else 1
    n_steps = n_blk // bps
    tq = bps * NA_Q_ROWS * GRID_W
    tk = NA_K_ROWS * GRID_W
    cb = (batch * seq) // ctx_len
    k0, v0 = d_model // hd, 2 * d_model // hd
    pairs = _na_pair_blocks(rpb)
    n_pairs = pairs.shape[1]
    out_shape = jax.ShapeDtypeStruct((m, d_model), BF16)

    lat = pl.pallas_call(
        functools.partial(_na_kernel, grid_rows=grid_rows, n_blk=n_blk, blocks_per_step=bps,
                          key_chunk=min(tk, NA_KEY_CHUNK), plan=_na_table_plan(grid_rows)),
        grid=(heads, batch, n_steps),
        in_specs=[pl.BlockSpec((tq, hd), lambda h, b, j: (b * n_steps + j, h)),
                  pl.BlockSpec((seq, hd), lambda h, b, j: (b, k0 + h)),
                  pl.BlockSpec((seq, hd), lambda h, b, j: (b, v0 + h)),
                  pl.BlockSpec((ctx_len, hd), lambda h, b, j: (cb + b, k0 + h)),
                  pl.BlockSpec((ctx_len, hd), lambda h, b, j: (cb + b, v0 + h)),
                  pl.BlockSpec((1, n_pairs, GRID_W, 2 * GRID_W), lambda h, b, j: (h, 0, 0, 0)),
                  pl.BlockSpec(memory_space=pl.ANY)],
        out_specs=pl.BlockSpec((tq, hd), lambda h, b, j: (b * n_steps + j, h)),
        out_shape=out_shape,
        scratch_shapes=[pltpu.VMEM((3, NA_Q_ROWS * GRID_W, tk), F32)],
        input_output_aliases={6: 0},
        compiler_params=_cp("arbitrary", "arbitrary", "arbitrary"),
        name="na_attn_latent",
    )(qkv, qkv, qkv, qkv, qkv, pairs, jnp.zeros(out_shape.shape, BF16))
    if not ctx_out:
        return lat
    return pl.pallas_call(
        _attn_small_kernel,
        grid=(batch, heads),
        in_specs=[pl.BlockSpec((ctx_len, hd), lambda b, h: (cb + b, h)),
                  pl.BlockSpec((ctx_len, hd), lambda b, h: (cb + b, k0 + h)),
                  pl.BlockSpec((ctx_len, hd), lambda b, h: (cb + b, v0 + h)),
                  pl.BlockSpec(memory_space=pl.ANY)],
        out_specs=pl.BlockSpec((ctx_len, hd), lambda b, h: (cb + b, h)),
        out_shape=out_shape,
        input_output_aliases={3: 0},
        compiler_params=_cp("parallel", "parallel"),
        name="na_attn_ctx",
    )(qkv, qkv, qkv, lat)


def _rope_tables(seq, pad_rows):
    pos = np.arange(seq)
    n_freq = MLA_ROPE // 4
    inv_freq = jnp.power(ROPE_THETA, -jnp.arange(n_freq, dtype=F32) / n_freq)
    rows = jnp.asarray(pos // GRID_W, F32)[:, None] * inv_freq
    cols = jnp.asarray(pos % GRID_W, F32)[:, None] * inv_freq
    ang = jnp.concatenate([rows, rows, cols, cols], axis=-1)
    sign = np.tile(np.concatenate([-np.ones(n_freq), np.ones(n_freq)]), 2).astype(np.float32)
    cos_t = jnp.concatenate([jnp.cos(ang), jnp.zeros((seq, LANES - MLA_ROPE), F32)], axis=1)
    sin_t = jnp.concatenate([jnp.sin(ang) * sign, jnp.zeros((seq, LANES - MLA_ROPE), F32)], axis=1)
    ident = jnp.concatenate([jnp.ones((pad_rows, MLA_ROPE), F32), jnp.zeros((pad_rows, LANES - MLA_ROPE), F32)], 1)
    return (jnp.concatenate([cos_t, ident], axis=0),
            jnp.concatenate([sin_t, jnp.zeros((pad_rows, LANES), F32)], axis=0))


def _rope_partner_perm():
    n_freq = MLA_ROPE // 4
    dd = np.arange(MLA_ROPE)
    return np.where(dd % (2 * n_freq) < n_freq, dd + n_freq, dd - n_freq)


def kernel(x, c, ctx, c_ctx, ada_w1, ada_w2, ada_b, norm_mix, norm_ffn, ffn_w_gate, ffn_w_up, ffn_w_down,
           hyb_w_in, ssd_conv_w, ssd_conv_b, ssd_dt_bias, ssd_a_log, ssd_d, ssd_norm,
           mla_q_norm, mla_w_q_up, mla_kv_norm, mla_w_kv_up, hyb_w_out,
           na_w_qkv, na_rpb, na_w_out, final_norm):
    batch, seq, d_model = x.shape
    ctx_len = ctx.shape[1]
    depth = ada_w1.shape[0]
    n_lat = batch * seq
    m = n_lat + batch * ctx_len
    d_inner = ssd_norm.shape[-1]
    heads = d_inner // SSD_HEAD_DIM
    conv_dim = ssd_conv_w.shape[1]
    q_rank = mla_q_norm.shape[-1]
    kv_rank = mla_kv_norm.shape[-1]
    tseg = int(np.gcd(seq, batch * ctx_len))
    tseg = min(tseg, 1024)
    tm_big = tseg
    tm_wide = _row_tile(m, WIDE_ROW_TILE)
    tr = min(tseg, 256)

    def seg(row0):
        return jnp.where(row0 < n_lat, row0 // seq, batch)

    h = jnp.concatenate([x.reshape(n_lat, d_model), ctx.reshape(batch * ctx_len, d_model)], axis=0)
    cond = jnp.zeros((COND_ROWS, d_model), F32).at[:batch].set(c).at[batch].set(c_ctx)
    mods_all = _ada(cond, ada_w1, ada_w2, ada_b).reshape(depth, COND_ROWS * N_MOD, 1, d_model)

    dt_cols = _round_up(2 * heads, LANES)
    perm = _rope_partner_perm()
    cos_t, sin_t = _rope_tables(seq, tm_big)
    lat_tiles = n_lat // tm_big
    tiles_per_seq = seq // tm_big

    def rope_table_idx(i):
        return jnp.where(i < lat_tiles, i % tiles_per_seq, tiles_per_seq)

    for layer in range(depth):
        ctx_out = layer < depth - 1
        rows = m if ctx_out else n_lat
        mods = mods_all[layer]
        u = _norm_rows(h, d_model, 0, norm_mix[layer], BF16, m, tr, mods, (0, 1), seg)
        if layer % 2 == 0:
            e = layer // 2
            w_in = hyb_w_in[e]
            s0 = d_inner + conv_dim
            s1 = s0 + 2 * heads
            s2 = s1 + q_rank
            s3 = s2 + kv_rank
            w_kpe = w_in[:, s3:]
            w_p2 = jnp.concatenate([w_in[:, s1:s2], w_in[:, s2:s3],
                                    jnp.pad(w_in[:, s0:s1], ((0, 0), (0, dt_cols - 2 * heads))),
                                    w_kpe, w_kpe[:, perm]], axis=1).astype(BF16)
            kv_idx = q_rank // kv_rank
            dt_idx = (q_rank + kv_rank) // dt_cols
            kpe_idx = (q_rank + kv_rank + dt_cols) // LANES
            p1 = _mm(u, hyb_w_in, BF16, tm_wide, _tile(s0, 512), layer=e, n=s0,
                     single_buffer_x=True)
            p2 = _mm(u, w_p2, F32, tm_big, _tile(w_p2.shape[1], 1024))
            xbc = _conv_silu(p1, d_inner, conv_dim, ssd_conv_w[e], ssd_conv_b[e], n_lat, seq, ctx_len,
                             min(ctx_len, 256), _tile(int(np.gcd(conv_dim, d_inner)), 2048))
            y2 = _ssd_scan(xbc, p2, dt_idx, ssd_dt_bias[e], ssd_a_log[e], batch, seq, ctx_len, d_inner)
            ssd_out = _ssd_finish(y2, xbc, p1, ssd_d[e], ssd_norm[e], d_inner, tr)

            qn = _norm_rows(p2, q_rank, 0, mla_q_norm[e], BF16, m, tr)
            kvn = _norm_rows(p2, kv_rank, kv_idx, mla_kv_norm[e], BF16, m, tr)
            wq = mla_w_q_up[e].reshape(q_rank, MLA_HEADS, MLA_NOPE + MLA_ROPE)
            wq_rope = wq[:, :, MLA_NOPE:]
            wq3 = jnp.concatenate([wq[:, :, :MLA_NOPE], wq_rope, wq_rope[:, :, perm]], axis=2)
            wq3 = wq3.reshape(q_rank, MLA_HEADS * 2 * LANES).astype(BF16)
            qh = _mm_qrope(qn, wq3, cos_t, sin_t, tm_big, min(4, MLA_HEADS), rope_table_idx)
            kv = _mm(kvn, mla_w_kv_up[e].astype(BF16), BF16, tm_big, _tile(MLA_HEADS * 2 * LANES, 2048))
            attn = _mla_attention(qh, kv, p2, kpe_idx, cos_t, sin_t, batch, seq, ctx_len, min(seq, MLA_Q_TILE))
            w_out = hyb_w_out[e].astype(BF16)
            h = _mm_res([ssd_out, attn], w_out, h, mods, 2, seg, rows, tm_big, _tile(d_model, 512),
                        single_buffer_x=True)
        else:
            o = layer // 2
            qkv = _mm(u, na_w_qkv, BF16, tm_wide, _tile(d_model, 512), layer=o, single_buffer_x=True,
                      scaled_cols=d_model, out_scale=float((d_model // NA_HEADS) ** -0.5 * LOG2_E))
            att = _na_attention(qkv, na_rpb[o], batch, seq, ctx_len, d_model, ctx_out)
            h = _mm_res([att], na_w_out[o].astype(BF16), h, mods, 2, seg, rows, tm_big, _tile(d_model, 512))
        u2 = _norm_rows(h, d_model, 0, norm_ffn[layer], BF16, rows, tr, mods, (3, 4), seg)
        hidden = _mm_swiglu(u2, ffn_w_gate, ffn_w_up, layer, rows, _row_tile(rows, WIDE_ROW_TILE),
                            _tile(ffn_w_gate.shape[-1], 256))
        h = _mm_res([hidden], ffn_w_down[layer].astype(BF16), h, mods, 5, seg, rows, tm_big,
                    _tile(d_model, 256), single_buffer_x=True)
    out = _norm_rows(h, d_model, 0, final_norm, F32, n_lat, tr)
    return out.reshape(batch, seq, d_model)
```

```python
import functools

import numpy as np
import jax
import jax.numpy as jnp
from jax import lax
from jax.experimental import pallas as pl
from jax.experimental.pallas import tpu as pltpu

F32 = jnp.float32
BF16 = jnp.bfloat16

NORM_EPS = 1e-6
N_MOD = 6
GRID_W = 64
ROPE_THETA = 10000.0
SSD_HEAD_DIM = 64
SSD_GROUPS = 8
SSD_STATE = 128
SSD_CONV = 5
SSD_CHUNK = 128
MLA_HEADS = 32
MLA_NOPE = 128
MLA_ROPE = 64
MLA_V = 128
NA_HEADS = 32
NA_WIN_H = 8
NA_WIN_W = 16
NA_Q_ROWS = 8
NA_K_ROWS = 16
NA_BLOCKS_PER_STEP = 2
NA_KEY_CHUNK = 256
MASK_NEG = -1e30
LOG2_E = 1.4426950408889634
MLA_KEY_CHUNK = 256
MLA_Q_TILE = 1024
LANES = 128
BF16_SUBLANES = 16
VMEM_LIMIT_BYTES = 56 * 2 ** 20
COND_ROWS = 16
WIDE_ROW_TILE = 2304


def _cp(*sem):
    return pltpu.CompilerParams(dimension_semantics=sem, vmem_limit_bytes=VMEM_LIMIT_BYTES)


def _tile(n, pref):
    if n <= pref:
        return n
    t = (pref // LANES) * LANES
    while t >= LANES:
        if n % t == 0:
            return t
        t -= LANES
    raise ValueError(f"no lane-aligned tile for {n}")


def _silu(v):
    return v * jax.nn.sigmoid(v)


def _ada_kernel(c_ref, w1_ref, w2_ref, b_ref, o_ref):
    cnd = c_ref[...]
    t = jnp.dot(_silu(cnd).astype(BF16), w1_ref[0].astype(BF16), preferred_element_type=F32)
    o_ref[0] = jnp.dot(t.astype(BF16), w2_ref[0].astype(BF16), preferred_element_type=F32) + b_ref[0]


def _ada(cond, w1, w2, bias):
    depth, d, r = w1.shape
    n = w2.shape[-1]
    tn = _tile(n, 2048)
    return pl.pallas_call(
        _ada_kernel,
        grid=(depth, n // tn),
        in_specs=[pl.BlockSpec((COND_ROWS, d), lambda l, j: (0, 0)),
                  pl.BlockSpec((1, d, r), lambda l, j: (l, 0, 0)),
                  pl.BlockSpec((1, r, tn), lambda l, j: (l, 0, j)),
                  pl.BlockSpec((1, 1, tn), lambda l, j: (l, 0, j))],
        out_specs=pl.BlockSpec((1, COND_ROWS, tn), lambda l, j: (l, 0, j)),
        out_shape=jax.ShapeDtypeStruct((depth, COND_ROWS, n), F32),
        compiler_params=_cp("arbitrary", "arbitrary"),
        name="ada",
    )(cond, w1, w2, bias.reshape(depth, 1, n))


def _norm_kernel(x_ref, g_ref, *rest, modulated):
    o_ref = rest[-1]
    xv = x_ref[...].astype(F32)
    y = xv * lax.rsqrt(jnp.mean(xv * xv, axis=-1, keepdims=True) + NORM_EPS) * g_ref[...]
    if modulated:
        sh_ref, sc_ref = rest[0], rest[1]
        y = y * (1.0 + sc_ref[0]) + sh_ref[0]
    o_ref[...] = y.astype(o_ref.dtype)


def _norm_rows(xa, width, col_idx, gain, out_dtype, n_rows, tr, mods=None, which=None, seg=None):
    in_specs = [pl.BlockSpec((tr, width), lambda i: (i, col_idx)),
                pl.BlockSpec((1, width), lambda i: (0, 0))]
    args = [xa, gain.reshape(1, width).astype(F32)]
    if mods is not None:
        sh_w, sc_w = which
        in_specs += [pl.BlockSpec((1, 1, width), lambda i: (seg(i * tr) * N_MOD + sh_w, 0, 0)),
                     pl.BlockSpec((1, 1, width), lambda i: (seg(i * tr) * N_MOD + sc_w, 0, 0))]
        args += [mods, mods]
    return pl.pallas_call(
        functools.partial(_norm_kernel, modulated=mods is not None),
        grid=(n_rows // tr,),
        in_specs=in_specs,
        out_specs=pl.BlockSpec((tr, width), lambda i: (i, 0)),
        out_shape=jax.ShapeDtypeStruct((n_rows, width), out_dtype),
        compiler_params=_cp("parallel"),
        name="norm",
    )(*args)


def _x_spec(tm, k, single_buffer):
    if single_buffer:
        return pl.BlockSpec((tm, k), lambda i, j: (i, 0), pipeline_mode=pl.Buffered(1))
    return pl.BlockSpec((tm, k), lambda i, j: (i, 0))


def _row_tile(rows, pref):
    t = (min(pref, rows) // BF16_SUBLANES) * BF16_SUBLANES
    while rows % t:
        t -= BF16_SUBLANES
    return t


def _w_spec(w, layer, k, tn):
    if w.ndim == 3:
        return pl.BlockSpec((None, k, tn), lambda i, j: (layer, 0, j))
    return pl.BlockSpec((k, tn), lambda i, j: (0, j))


def _mm_kernel(x_ref, w_ref, o_ref, *, scaled_tiles, out_scale):
    acc = jnp.dot(x_ref[...], w_ref[...].astype(BF16), preferred_element_type=F32)
    if scaled_tiles:
        acc = acc * jnp.where(pl.program_id(1) < scaled_tiles, out_scale, 1.0)
    o_ref[...] = acc.astype(o_ref.dtype)


def _mm(xa, w, out_dtype, tm, tn, layer=None, n=None, scaled_cols=0, out_scale=1.0, single_buffer_x=False):
    m, k = xa.shape
    n = w.shape[-1] if n is None else n
    return pl.pallas_call(
        functools.partial(_mm_kernel, scaled_tiles=scaled_cols // tn, out_scale=out_scale),
        grid=(m // tm, n // tn),
        in_specs=[_x_spec(tm, k, single_buffer_x), _w_spec(w, layer, k, tn)],
        out_specs=pl.BlockSpec((tm, tn), lambda i, j: (i, j)),
        out_shape=jax.ShapeDtypeStruct((m, n), out_dtype),
        compiler_params=_cp("parallel", "arbitrary"),
        name="mm",
    )(xa, w)


def _mm_swiglu_kernel(x_ref, wg_ref, wu_ref, o_ref):
    xv = x_ref[...]
    gate = jnp.dot(xv, wg_ref[...].astype(BF16), preferred_element_type=F32)
    up = jnp.dot(xv, wu_ref[...].astype(BF16), preferred_element_type=F32)
    o_ref[...] = (_silu(gate) * up).astype(o_ref.dtype)


def _mm_swiglu(xa, wg, wu, layer, rows, tm, tn):
    m, k = xa.shape
    n = wg.shape[-1]
    return pl.pallas_call(
        _mm_swiglu_kernel,
        grid=(rows // tm, n // tn),
        in_specs=[_x_spec(tm, k, True), _w_spec(wg, layer, k, tn), _w_spec(wu, layer, k, tn)],
        out_specs=pl.BlockSpec((tm, tn), lambda i, j: (i, j)),
        out_shape=jax.ShapeDtypeStruct((m, n), BF16),
        compiler_params=_cp("parallel", "arbitrary"),
        name="mm_swiglu",
    )(xa, wg, wu)


def _mm_res_kernel(*refs, n_pairs):
    o_ref = refs[-1]
    res_ref, gate_ref = refs[2 * n_pairs], refs[2 * n_pairs + 1]
    acc = None
    for i in range(n_pairs):
        part = jnp.dot(refs[i][...], refs[n_pairs + i][...], preferred_element_type=F32)
        acc = part if acc is None else acc + part
    o_ref[...] = res_ref[...] + gate_ref[0] * acc


def _mm_res(xas, w, layer, res, mods, which, seg, rows, tm, tn, single_buffer_x=False):
    m, n = res.shape
    n_pairs = len(xas)
    k = xas[0].shape[1]
    in_specs = [_x_spec(tm, k, single_buffer_x) for _ in xas]
    in_specs += [pl.BlockSpec((None, k, tn), functools.partial(lambda i, j, p: (layer, p, j), p=p))
                 for p in range(n_pairs)]
    in_specs += [pl.BlockSpec((tm, tn), lambda i, j: (i, j)),
                 pl.BlockSpec((1, 1, tn), lambda i, j: (seg(i * tm) * N_MOD + which, 0, j))]
    return pl.pallas_call(
        functools.partial(_mm_res_kernel, n_pairs=n_pairs),
        grid=(rows // tm, n // tn),
        in_specs=in_specs,
        out_specs=pl.BlockSpec((tm, tn), lambda i, j: (i, j)),
        out_shape=jax.ShapeDtypeStruct((m, n), F32),
        compiler_params=_cp("parallel", "arbitrary"),
        name="mm_res",
    )(*xas, *([w] * n_pairs), res, mods)


def _mm_qrope_kernel(x_ref, w_ref, cos_ref, sin_ref, o_ref, *, heads, out_scale):
    acc = jnp.dot(x_ref[...], w_ref[...], preferred_element_type=F32) * out_scale
    cs, sn = cos_ref[...], sin_ref[...]
    for hh in range(heads):
        base = hh * 2 * LANES
        o_ref[:, base:base + LANES] = acc[:, base:base + LANES].astype(o_ref.dtype)
        half = acc[:, base + LANES:base + 2 * LANES]
        rot = half * cs + pltpu.roll(half, MLA_ROPE, 1) * sn
        o_ref[:, base + LANES:base + 2 * LANES] = rot.astype(o_ref.dtype)


def _mm_qrope(xa, w, cos_t, sin_t, tm, heads_per_step, table_idx):
    m, k = xa.shape
    n = w.shape[1]
    tn = heads_per_step * 2 * LANES
    out_scale = float((MLA_NOPE + MLA_ROPE) ** -0.5 * LOG2_E)
    return pl.pallas_call(
        functools.partial(_mm_qrope_kernel, heads=heads_per_step, out_scale=out_scale),
        grid=(m // tm, n // tn),
        in_specs=[pl.BlockSpec((tm, k), lambda i, j: (i, 0)),
                  pl.BlockSpec((k, tn), lambda i, j: (0, j)),
                  pl.BlockSpec((tm, LANES), lambda i, j: (table_idx(i), 0)),
                  pl.BlockSpec((tm, LANES), lambda i, j: (table_idx(i), 0))],
        out_specs=pl.BlockSpec((tm, tn), lambda i, j: (i, j)),
        out_shape=jax.ShapeDtypeStruct((m, n), BF16),
        compiler_params=_cp("parallel", "arbitrary"),
        name="mm_qrope",
    )(xa, w, cos_t, sin_t)


def _conv_kernel(prev_ref, cur_ref, next_ref, w_ref, b_ref, o_ref, *, n_lat_blocks, blocks_per_seq, blocks_per_ctx):
    p = pl.program_id(0)
    is_ctx = p >= n_lat_blocks
    q = jnp.where(is_ctx, (p - n_lat_blocks) % blocks_per_ctx, p % blocks_per_seq)
    q_last = jnp.where(is_ctx, blocks_per_ctx - 1, blocks_per_seq - 1)
    keep_prev = jnp.where(q == 0, 0.0, 1.0)
    keep_next = jnp.where(q == q_last, 0.0, 1.0)
    xc = cur_ref[...].astype(F32)
    rows = xc.shape[0]
    halo = prev_ref.shape[0]
    full = jnp.concatenate([prev_ref[...].astype(F32) * keep_prev, xc,
                            next_ref[...].astype(F32) * keep_next], axis=0)
    total = rows + 2 * halo
    wv = w_ref[...]
    acc = jnp.zeros_like(xc) + b_ref[...]
    for tap in range(SSD_CONV):
        shift = (SSD_CONV // 2 - tap) % total
        moved = full if shift == 0 else pltpu.roll(full, shift, 0)
        acc = acc + wv[tap:tap + 1, :] * moved[halo:halo + rows]
    o_ref[...] = _silu(acc).astype(o_ref.dtype)


def _conv_silu(p1, col0, width, conv_w, conv_b, n_lat_rows, seq, ctx_len, rb, tc):
    m = p1.shape[0]
    halo = BF16_SUBLANES
    cb0 = col0 // tc
    per = rb // halo
    last_halo = m // halo - 1
    return pl.pallas_call(
        functools.partial(_conv_kernel, n_lat_blocks=n_lat_rows // rb, blocks_per_seq=seq // rb,
                          blocks_per_ctx=ctx_len // rb),
        grid=(m // rb, width // tc),
        in_specs=[pl.BlockSpec((halo, tc), lambda p, j: (jnp.maximum(p * per - 1, 0), cb0 + j)),
                  pl.BlockSpec((rb, tc), lambda p, j: (p, cb0 + j)),
                  pl.BlockSpec((halo, tc), lambda p, j: (jnp.minimum((p + 1) * per, last_halo), cb0 + j)),
                  pl.BlockSpec((SSD_CONV, tc), lambda p, j: (0, j)),
                  pl.BlockSpec((1, tc), lambda p, j: (0, j))],
        out_specs=pl.BlockSpec((rb, tc), lambda p, j: (p, j)),
        out_shape=jax.ShapeDtypeStruct((m, width), BF16),
        compiler_params=_cp("parallel", "arbitrary"),
        name="conv_silu",
    )(p1, p1, p1, jnp.transpose(conv_w).astype(F32), conv_b.reshape(1, width).astype(F32))


def _split3(v):
    hi = v.astype(BF16)
    r1 = v - hi.astype(F32)
    mid = r1.astype(BF16)
    lo = (r1 - mid.astype(F32)).astype(BF16)
    return hi, mid, lo


def _dot_exact_rhs(sel, v):
    return sum(jnp.dot(sel, part, preferred_element_type=F32) for part in _split3(v))


def _dot_exact_lhs(v, sel):
    return sum(jnp.dot(part, sel, preferred_element_type=F32) for part in _split3(v))


def _ssd_kernel(xs_ref, b_ref, c_ref, dt_ref, bias_ref, alog_ref, y_ref, st_ref, ex_ref, *, heads,
                heads_per_group):
    d = pl.program_id(1)
    t = xs_ref.shape[0]
    hd = SSD_HEAD_DIM
    n = SSD_STATE
    width = heads_per_group * hd
    ncol = dt_ref.shape[1]

    @pl.when(pl.program_id(2) == 0)
    def _():
        st_ref[...] = jnp.zeros_like(st_ref)
        ri = lax.broadcasted_iota(jnp.int32, ex_ref.shape, 0)
        ci = lax.broadcasted_iota(jnp.int32, ex_ref.shape, 1)
        ex_ref[...] = (ri == d * heads + ci // hd).astype(BF16)

    xv = dt_ref[...] + bias_ref[...]
    dtv = jnp.maximum(xv, 0.0) + jnp.log1p(jnp.exp(-jnp.abs(xv)))
    da = dtv * (-jnp.exp(alog_ref[...]))
    qi = lax.broadcasted_iota(jnp.int32, (t, t), 0)
    si = lax.broadcasted_iota(jnp.int32, (t, t), 1)
    mask = (si - qi) * jnp.where(d == 0, 1, -1) <= 0
    acum = _dot_exact_rhs(mask.astype(BF16), da)

    expand = ex_ref[...]
    acum_x = _dot_exact_lhs(acum, expand)
    dtv_x = _dot_exact_lhs(dtv, expand)
    acum_t_all = jnp.transpose(acum)
    acum_t = jnp.where(d == 0, acum_t_all[0:heads], acum_t_all[heads:2 * heads])
    a_end = jnp.where(d == 0, acum_x[t - 1:t, :], acum_x[0:1, :])

    xdt = xs_ref[...].astype(F32) * dtv_x
    xdt_b = xdt.astype(BF16)
    scale_q = jnp.exp(acum_x)
    xw = (xdt * jnp.exp(a_end - acum_x)).astype(BF16)
    decay_end = jnp.exp(a_end)

    lane = lax.broadcasted_iota(jnp.int32, (t, LANES), 1)
    for g in range(SSD_GROUPS):
        gs = slice(g * width, (g + 1) * width)
        cg = c_ref[:, g * n:(g + 1) * n]
        bg = b_ref[:, g * n:(g + 1) * n]
        cb = lax.dot_general(cg, bg, (((1,), (1,)), ((), ())), preferred_element_type=F32)
        state = st_ref[g]
        y_state = jnp.dot(cg, state.astype(BF16), preferred_element_type=F32)
        bg_t = jnp.transpose(bg.astype(F32)).astype(BF16)
        st_ref[g] = state * decay_end[:, gs] + jnp.dot(bg_t, xw[:, gs], preferred_element_type=F32)
        for p in range(heads_per_group // 2):
            sl = slice(g * width + p * LANES, g * width + (p + 1) * LANES)
            mats = []
            for hh in (g * heads_per_group + 2 * p, g * heads_per_group + 2 * p + 1):
                a_q = acum_x[:, hh * hd:hh * hd + 1]
                a_s = acum_t[hh:hh + 1, :]
                mats.append(cb * jnp.exp(jnp.where(mask, a_q - a_s, -jnp.inf)))
            lcat = jnp.concatenate(mats, axis=1).astype(BF16)
            xp = xdt_b[:, sl]
            xcat = jnp.concatenate([jnp.where(lane < hd, xp, jnp.zeros_like(xp)),
                                    jnp.where(lane >= hd, xp, jnp.zeros_like(xp))], axis=0)
            y_in = jnp.dot(lcat, xcat, preferred_element_type=F32)
            y_ref[0, :, sl] = (y_in + y_state[:, p * LANES:(p + 1) * LANES] * scale_q[:, sl]).astype(y_ref.dtype)


def _ssd_scan(xbc, p2, dt_col_idx, dt_bias, a_log, batch, seq, ctx_len, d_inner):
    m = xbc.shape[0]
    t = SSD_CHUNK
    heads = d_inner // SSD_HEAD_DIM
    hpg = heads // SSD_GROUPS
    width = hpg * SSD_HEAD_DIM
    n = SSD_STATE
    nc_ctx, nc_lat = ctx_len // t, seq // t
    nch = nc_ctx + nc_lat
    ctx_base = batch * nc_lat
    ncol = _round_up(2 * heads, LANES)

    def row_block(b, d, i):
        ctx_blk = ctx_base + b * nc_ctx + jnp.where(d == 0, i, nc_ctx - 1 - i)
        lat_blk = b * nc_lat + jnp.where(d == 0, i - nc_ctx, nch - 1 - i)
        return jnp.where(i < nc_ctx, ctx_blk, lat_blk)

    gn = SSD_GROUPS * n
    b_col = d_inner // gn
    pad = ncol - 2 * heads
    bias = jnp.pad(dt_bias.reshape(1, 2 * heads).astype(F32), ((0, 0), (0, pad)))
    alog = jnp.pad(a_log.reshape(1, 2 * heads).astype(F32), ((0, 0), (0, pad)))
    return pl.pallas_call(
        functools.partial(_ssd_kernel, heads=heads, heads_per_group=hpg),
        grid=(batch, 2, nch),
        in_specs=[pl.BlockSpec((t, d_inner), lambda b, d, i: (row_block(b, d, i), 0)),
                  pl.BlockSpec((t, gn), lambda b, d, i: (row_block(b, d, i), b_col)),
                  pl.BlockSpec((t, gn), lambda b, d, i: (row_block(b, d, i), b_col + 1)),
                  pl.BlockSpec((t, ncol), lambda b, d, i: (row_block(b, d, i), dt_col_idx)),
                  pl.BlockSpec((1, ncol), lambda b, d, i: (0, 0)),
                  pl.BlockSpec((1, ncol), lambda b, d, i: (0, 0))],
        out_specs=pl.BlockSpec((1, t, d_inner), lambda b, d, i: (d, row_block(b, d, i), 0)),
        out_shape=jax.ShapeDtypeStruct((2, m, d_inner), BF16),
        scratch_shapes=[pltpu.VMEM((SSD_GROUPS, n, width), F32), pltpu.VMEM((ncol, d_inner), BF16)],
        compiler_params=_cp("parallel", "parallel", "arbitrary"),
        name="ssd_scan",
    )(xbc, xbc, xbc, p2, bias, alog)


def _round_up(v, mult):
    return -(-v // mult) * mult


def _ssd_finish_kernel(y_ref, xs_ref, z_ref, d_ref, g_ref, o_ref):
    y = y_ref[0].astype(F32) + y_ref[1].astype(F32) + d_ref[...] * xs_ref[...].astype(F32)
    zv = z_ref[...].astype(F32)
    y = y * _silu(zv)
    gw = y.shape[1] // SSD_GROUPS
    for gi in range(SSD_GROUPS):
        sl = slice(gi * gw, (gi + 1) * gw)
        yg = y[:, sl]
        inv = lax.rsqrt(jnp.mean(yg * yg, axis=-1, keepdims=True) + NORM_EPS)
        o_ref[:, sl] = (yg * inv * g_ref[:, sl]).astype(o_ref.dtype)


def _ssd_finish(y2, xbc, p1, d_skip, ssd_norm, d_inner, tr):
    m = xbc.shape[0]
    d_x = jnp.repeat(d_skip.astype(F32), SSD_HEAD_DIM).reshape(1, d_inner)
    return pl.pallas_call(
        _ssd_finish_kernel,
        grid=(m // tr,),
        in_specs=[pl.BlockSpec((2, tr, d_inner), lambda i: (0, i, 0)),
                  pl.BlockSpec((tr, d_inner), lambda i: (i, 0)),
                  pl.BlockSpec((tr, d_inner), lambda i: (i, 0)),
                  pl.BlockSpec((1, d_inner), lambda i: (0, 0)),
                  pl.BlockSpec((1, d_inner), lambda i: (0, 0))],
        out_specs=pl.BlockSpec((tr, d_inner), lambda i: (i, 0)),
        out_shape=jax.ShapeDtypeStruct((m, d_inner), BF16),
        compiler_params=_cp("parallel"),
        name="ssd_finish",
    )(y2, xbc, p1, d_x, ssd_norm.reshape(1, d_inner).astype(F32))


def _mla_attn_kernel(*refs, ctx_len, with_latent, key_chunk):
    if with_latent:
        q_ref, kc_ref, vc_ref, pc_ref, kl_ref, vl_ref, pl_ref, cos_ref, sin_ref, o_ref, kt_ref, vs_ref = refs
        seq = kl_ref.shape[0]
    else:
        q_ref, kc_ref, vc_ref, pc_ref, _, o_ref, kt_ref, vs_ref = refs
        seq = 0

    @pl.when(pl.program_id(2) == 0)
    def _():
        lane = lax.broadcasted_iota(jnp.int32, (ctx_len, LANES), 1)
        kt_ref[0:LANES, 0:ctx_len] = jnp.transpose(kc_ref[...].astype(F32)).astype(BF16)
        kt_ref[LANES:, 0:ctx_len] = jnp.transpose(jnp.where(lane < MLA_ROPE, pc_ref[...], 0.0)).astype(BF16)
        vs_ref[0:ctx_len, :] = vc_ref[...]
        for a in range(0, seq, key_chunk):
            rows = slice(a, a + key_chunk)
            cols = slice(ctx_len + a, ctx_len + a + key_chunk)
            kt_ref[0:LANES, cols] = jnp.transpose(kl_ref[rows, :].astype(F32)).astype(BF16)
            blk = pl_ref[rows, :]
            rot = blk * cos_ref[rows, :] + pltpu.roll(blk, MLA_ROPE, 1) * sin_ref[rows, :]
            kt_ref[LANES:, cols] = jnp.transpose(rot).astype(BF16)
            vs_ref[cols, :] = vl_ref[rows, :]

    qv = q_ref[...]
    bounds = [(0, ctx_len)] + [(ctx_len + a, ctx_len + a + key_chunk) for a in range(0, seq, key_chunk)]
    m_run = acc = l_part = None
    for lo, hi in bounds:
        s = jnp.dot(qv, kt_ref[:, lo:hi], preferred_element_type=F32)
        m_new = jnp.max(s, axis=-1, keepdims=True)
        if m_run is not None:
            m_new = jnp.maximum(m_run, m_new)
        p = jnp.exp2(s - m_new)
        l_new = p[:, 0:LANES]
        for c in range(LANES, hi - lo, LANES):
            l_new = l_new + p[:, c:c + LANES]
        pv = jnp.dot(p.astype(BF16), vs_ref[lo:hi, :], preferred_element_type=F32)
        if m_run is None:
            acc, l_part = pv, l_new
        else:
            alpha = jnp.exp2(m_run - m_new)
            acc = alpha * acc + pv
            l_part = alpha * l_part + l_new
        m_run = m_new
    o_ref[...] = (acc / jnp.sum(l_part, axis=-1, keepdims=True)).astype(o_ref.dtype)


def _mla_attention(qh, kv, p2, kpe_idx, cos_t, sin_t, batch, seq, ctx_len, tq):
    m = qh.shape[0]
    heads = MLA_HEADS
    key_chunk = min(seq, MLA_KEY_CHUNK)
    cb = (batch * seq) // ctx_len
    nq = seq // tq
    out_shape = jax.ShapeDtypeStruct((m, heads * MLA_V), BF16)
    ctx_specs = [pl.BlockSpec((ctx_len, LANES), lambda b, h, i: (cb + b, 2 * h)),
                 pl.BlockSpec((ctx_len, LANES), lambda b, h, i: (cb + b, 2 * h + 1)),
                 pl.BlockSpec((ctx_len, LANES), lambda b, h, i: (cb + b, kpe_idx))]
    lat = pl.pallas_call(
        functools.partial(_mla_attn_kernel, ctx_len=ctx_len, with_latent=True, key_chunk=key_chunk),
        grid=(batch, heads, nq),
        in_specs=[pl.BlockSpec((tq, 2 * LANES), lambda b, h, i: (b * nq + i, h))] + ctx_specs + [
            pl.BlockSpec((seq, LANES), lambda b, h, i: (b, 2 * h)),
            pl.BlockSpec((seq, LANES), lambda b, h, i: (b, 2 * h + 1)),
            pl.BlockSpec((seq, LANES), lambda b, h, i: (b, kpe_idx)),
            pl.BlockSpec((seq, LANES), lambda b, h, i: (0, 0)),
            pl.BlockSpec((seq, LANES), lambda b, h, i: (0, 0))],
        out_specs=pl.BlockSpec((tq, MLA_V), lambda b, h, i: (b * nq + i, h)),
        out_shape=out_shape,
        scratch_shapes=[pltpu.VMEM((2 * LANES, ctx_len + seq), BF16),
                        pltpu.VMEM((ctx_len + seq, MLA_V), BF16)],
        compiler_params=_cp("parallel", "parallel", "arbitrary"),
        name="mla_attn_latent",
    )(qh, kv, kv, p2, kv, kv, p2, cos_t, sin_t)
    return pl.pallas_call(
        functools.partial(_mla_attn_kernel, ctx_len=ctx_len, with_latent=False, key_chunk=key_chunk),
        grid=(batch, heads, 1),
        in_specs=[pl.BlockSpec((ctx_len, 2 * LANES), lambda b, h, i: (cb + b, h))] + ctx_specs + [
            pl.BlockSpec(memory_space=pl.ANY)],
        out_specs=pl.BlockSpec((ctx_len, MLA_V), lambda b, h, i: (cb + b, h)),
        out_shape=out_shape,
        scratch_shapes=[pltpu.VMEM((2 * LANES, ctx_len), BF16), pltpu.VMEM((ctx_len, MLA_V), BF16)],
        input_output_aliases={4: 0},
        compiler_params=_cp("parallel", "parallel", "arbitrary"),
        name="mla_attn_ctx",
    )(qh, kv, kv, p2, lat)


def _na_kernel(q_ref, k_ref, v_ref, kc_ref, vc_ref, pair_ref, o_ref, tab_ref, *, grid_rows, n_blk,
               blocks_per_step, key_chunk, plan):
    step = pl.program_id(2)

    @pl.when(jnp.logical_and(pl.program_id(1) == 0, step == 0))
    def _():
        for case, per_row in enumerate(plan):
            for qa, per_pair in enumerate(per_row):
                for i, src in enumerate(per_pair):
                    if src < 0:
                        blk = jnp.full((GRID_W, 2 * GRID_W), MASK_NEG, F32)
                    else:
                        blk = pair_ref[0, src] * LOG2_E
                    tab_ref[case, qa * GRID_W:(qa + 1) * GRID_W, i * 2 * GRID_W:(i + 1) * 2 * GRID_W] = blk

    nt = (((1,), (1,)), ((), ()))
    tq = NA_Q_ROWS * GRID_W
    n_keys = NA_K_ROWS * GRID_W
    for sub in range(blocks_per_step):
        j = step * blocks_per_step + sub
        case = jnp.where(j == 0, 0, jnp.where(j == n_blk - 1, 2, 1))
        ws = jnp.clip(NA_Q_ROWS * j - NA_WIN_H // 2, 0, grid_rows - NA_K_ROWS)
        rows = pl.ds(sub * tq, tq)
        qv = q_ref[rows, :]
        m_run = acc = l_part = None
        for c0 in [None] + list(range(0, n_keys, key_chunk)):
            if c0 is None:
                s = lax.dot_general(qv, kc_ref[...], nt, preferred_element_type=F32)
                vv = vc_ref[...]
            else:
                krows = pl.ds(pl.multiple_of(ws * GRID_W + c0, GRID_W), key_chunk)
                s = (lax.dot_general(qv, k_ref[krows, :], nt, preferred_element_type=F32)
                     + tab_ref[case, :, c0:c0 + key_chunk])
                vv = v_ref[krows, :]
            m_new = jnp.max(s, axis=-1, keepdims=True)
            if m_run is not None:
                m_new = jnp.maximum(m_run, m_new)
            p = jnp.exp2(s - m_new)
            l_new = p[:, 0:LANES]
            for c in range(LANES, s.shape[1], LANES):
                l_new = l_new + p[:, c:c + LANES]
            pv = jnp.dot(p.astype(BF16), vv, preferred_element_type=F32)
            if m_run is None:
                acc, l_part = pv, l_new
            else:
                alpha = jnp.exp2(m_run - m_new)
                acc = alpha * acc + pv
                l_part = alpha * l_part + l_new
            m_run = m_new
        o_ref[rows, :] = (acc / jnp.sum(l_part, axis=-1, keepdims=True)).astype(o_ref.dtype)


def _attn_small_kernel(q_ref, k_ref, v_ref, _, o_ref):
    s = lax.dot_general(q_ref[...], k_ref[...], (((1,), (1,)), ((), ())), preferred_element_type=F32)
    p = jnp.exp2(s - jnp.max(s, axis=-1, keepdims=True))
    o = jnp.dot(p.astype(BF16), v_ref[...], preferred_element_type=F32)
    o_ref[...] = (o / jnp.sum(p, axis=-1, keepdims=True)).astype(o_ref.dtype)


def _na_pair_blocks(rpb):
    qc = np.arange(GRID_W)[:, None]
    kc = np.arange(GRID_W)[None, :]
    c0 = np.clip(qc - NA_WIN_W // 2, 0, GRID_W - NA_WIN_W)
    col_ok = (kc >= c0) & (kc < c0 + NA_WIN_W)
    dc = np.clip(kc - qc + NA_WIN_W - 1, 0, 2 * NA_WIN_W - 2)
    a = jnp.take(rpb.astype(F32), jnp.asarray(dc), axis=2)
    a = jnp.where(jnp.asarray(col_ok), a, MASK_NEG)
    neg = jnp.full_like(a, MASK_NEG)
    return jnp.concatenate([jnp.concatenate([a[:, :-1], a[:, 1:]], axis=-1),
                            jnp.concatenate([a, neg], axis=-1),
                            jnp.concatenate([neg, a], axis=-1)], axis=1)


def _na_table_plan(grid_rows):
    n_blk = grid_rows // NA_Q_ROWS
    n_dr = 2 * NA_WIN_H - 1

    def block_plan(j):
        ws = int(np.clip(NA_Q_ROWS * j - NA_WIN_H // 2, 0, grid_rows - NA_K_ROWS))
        rows = []
        for qa in range(NA_Q_ROWS):
            qr = NA_Q_ROWS * j + qa
            r0 = int(np.clip(qr - NA_WIN_H // 2, 0, grid_rows - NA_WIN_H))
            pairs = []
            for i in range(NA_K_ROWS // 2):
                kr0 = ws + 2 * i
                ok0 = r0 <= kr0 < r0 + NA_WIN_H
                ok1 = r0 <= kr0 + 1 < r0 + NA_WIN_H
                dr0 = kr0 - qr + NA_WIN_H - 1
                if ok0 and ok1:
                    pairs.append(dr0)
                elif ok0:
                    pairs.append(n_dr - 1 + dr0)
                elif ok1:
                    pairs.append(2 * n_dr - 1 + dr0 + 1)
                else:
                    pairs.append(-1)
            rows.append(tuple(pairs))
        return tuple(rows)

    plans = [block_plan(j) for j in range(n_blk)]
    cases = (plans[0], plans[min(1, n_blk - 1)], plans[n_blk - 1])
    for j in range(1, n_blk - 1):
        if plans[j] != cases[1]:
            raise ValueError("neighbourhood-attention middle blocks are not translation invariant")
    return cases


def _na_attention(qkv, rpb, batch, seq, ctx_len, d_model, ctx_out):
    m = qkv.shape[0]
    heads = NA_HEADS
    hd = d_model // heads
    grid_rows = seq // GRID_W
    n_blk = grid_rows // NA_Q_ROWS
    bps = NA_BLOCKS_PER_STEP if n_blk % NA_BLOCKS_PER_STEP == 0 else 1
    n_steps = n_blk // bps
    tq = bps * NA_Q_ROWS * GRID_W
    tk = NA_K_ROWS * GRID_W
    cb = (batch * seq) // ctx_len
    k0, v0 = d_model // hd, 2 * d_model // hd
    pairs = _na_pair_blocks(rpb)
    n_pairs = pairs.shape[1]
    out_shape = jax.ShapeDtypeStruct((m, d_model), BF16)

    lat = pl.pallas_call(
        functools.partial(_na_kernel, grid_rows=grid_rows, n_blk=n_blk, blocks_per_step=bps,
                          key_chunk=min(tk, NA_KEY_CHUNK), plan=_na_table_plan(grid_rows)),
        grid=(heads, batch, n_steps),
        in_specs=[pl.BlockSpec((tq, hd), lambda h, b, j: (b * n_steps + j, h)),
                  pl.BlockSpec((seq, hd), lambda h, b, j: (b, k0 + h)),
                  pl.BlockSpec((seq, hd), lambda h, b, j: (b, v0 + h)),
                  pl.BlockSpec((ctx_len, hd), lambda h, b, j: (cb + b, k0 + h)),
                  pl.BlockSpec((ctx_len, hd), lambda h, b, j: (cb + b, v0 + h)),
                  pl.BlockSpec((1, n_pairs, GRID_W, 2 * GRID_W), lambda h, b, j: (h, 0, 0, 0))],
        out_specs=pl.BlockSpec((tq, hd), lambda h, b, j: (b * n_steps + j, h)),
        out_shape=out_shape,
        scratch_shapes=[pltpu.VMEM((3, NA_Q_ROWS * GRID_W, tk), F32)],
        compiler_params=_cp("arbitrary", "arbitrary", "arbitrary"),
        name="na_attn_latent",
    )(qkv, qkv, qkv, qkv, qkv, pairs)
    if not ctx_out:
        return lat
    return pl.pallas_call(
        _attn_small_kernel,
        grid=(batch, heads),
        in_specs=[pl.BlockSpec((ctx_len, hd), lambda b, h: (cb + b, h)),
                  pl.BlockSpec((ctx_len, hd), lambda b, h: (cb + b, k0 + h)),
                  pl.BlockSpec((ctx_len, hd), lambda b, h: (cb + b, v0 + h)),
                  pl.BlockSpec(memory_space=pl.ANY)],
        out_specs=pl.BlockSpec((ctx_len, hd), lambda b, h: (cb + b, h)),
        out_shape=out_shape,
        input_output_aliases={3: 0},
        compiler_params=_cp("parallel", "parallel"),
        name="na_attn_ctx",
    )(qkv, qkv, qkv, lat)


def _rope_tables(seq, pad_rows):
    pos = np.arange(seq)
    n_freq = MLA_ROPE // 4
    inv_freq = jnp.power(ROPE_THETA, -jnp.arange(n_freq, dtype=F32) / n_freq)
    rows = jnp.asarray(pos // GRID_W, F32)[:, None] * inv_freq
    cols = jnp.asarray(pos % GRID_W, F32)[:, None] * inv_freq
    ang = jnp.concatenate([rows, rows, cols, cols], axis=-1)
    sign = np.tile(np.concatenate([-np.ones(n_freq), np.ones(n_freq)]), 2).astype(np.float32)
    cos_t = jnp.concatenate([jnp.cos(ang), jnp.zeros((seq, LANES - MLA_ROPE), F32)], axis=1)
    sin_t = jnp.concatenate([jnp.sin(ang) * sign, jnp.zeros((seq, LANES - MLA_ROPE), F32)], axis=1)
    ident = jnp.concatenate([jnp.ones((pad_rows, MLA_ROPE), F32), jnp.zeros((pad_rows, LANES - MLA_ROPE), F32)], 1)
    return (jnp.concatenate([cos_t, ident], axis=0),
            jnp.concatenate([sin_t, jnp.zeros((pad_rows, LANES), F32)], axis=0))


def _rope_partner_perm():
    n_freq = MLA_ROPE // 4
    dd = np.arange(MLA_ROPE)
    return np.where(dd % (2 * n_freq) < n_freq, dd + n_freq, dd - n_freq)


def kernel(x, c, ctx, c_ctx, ada_w1, ada_w2, ada_b, norm_mix, norm_ffn, ffn_w_gate, ffn_w_up, ffn_w_down,
           hyb_w_in, ssd_conv_w, ssd_conv_b, ssd_dt_bias, ssd_a_log, ssd_d, ssd_norm,
           mla_q_norm, mla_w_q_up, mla_kv_norm, mla_w_kv_up, hyb_w_out,
           na_w_qkv, na_rpb, na_w_out, final_norm):
    batch, seq, d_model = x.shape
    ctx_len = ctx.shape[1]
    depth = ada_w1.shape[0]
    n_lat = batch * seq
    m = n_lat + batch * ctx_len
    d_inner = ssd_norm.shape[-1]
    heads = d_inner // SSD_HEAD_DIM
    conv_dim = ssd_conv_w.shape[1]
    q_rank = mla_q_norm.shape[-1]
    kv_rank = mla_kv_norm.shape[-1]
    tseg = int(np.gcd(seq, batch * ctx_len))
    tseg = min(tseg, 1024)
    tm_big = tseg
    tm_wide = _row_tile(m, WIDE_ROW_TILE)
    tr = min(tseg, 256)

    def seg(row0):
        return jnp.where(row0 < n_lat, row0 // seq, batch)

    h = jnp.concatenate([x.reshape(n_lat, d_model), ctx.reshape(batch * ctx_len, d_model)], axis=0)
    cond = jnp.zeros((COND_ROWS, d_model), F32).at[:batch].set(c).at[batch].set(c_ctx)
    mods_all = _ada(cond, ada_w1, ada_w2, ada_b).reshape(depth, COND_ROWS * N_MOD, 1, d_model)

    dt_cols = _round_up(2 * heads, LANES)
    perm = _rope_partner_perm()
    cos_t, sin_t = _rope_tables(seq, tm_big)
    lat_tiles = n_lat // tm_big
    tiles_per_seq = seq // tm_big

    def rope_table_idx(i):
        return jnp.where(i < lat_tiles, i % tiles_per_seq, tiles_per_seq)

    ffn_w_down_b = ffn_w_down.astype(BF16)
    hyb_w_out_b = hyb_w_out.astype(BF16)
    na_w_out_b = na_w_out.astype(BF16)
    for layer in range(depth):
        ctx_out = layer < depth - 1
        rows = m if ctx_out else n_lat
        mods = mods_all[layer]
        u = _norm_rows(h, d_model, 0, norm_mix[layer], BF16, m, tr, mods, (0, 1), seg)
        if layer % 2 == 0:
            e = layer // 2
            w_in = hyb_w_in[e]
            s0 = d_inner + conv_dim
            s1 = s0 + 2 * heads
            s2 = s1 + q_rank
            s3 = s2 + kv_rank
            w_kpe = w_in[:, s3:]
            w_p2 = jnp.concatenate([w_in[:, s1:s2], w_in[:, s2:s3],
                                    jnp.pad(w_in[:, s0:s1], ((0, 0), (0, dt_cols - 2 * heads))),
                                    w_kpe, w_kpe[:, perm]], axis=1)
            kv_idx = q_rank // kv_rank
            dt_idx = (q_rank + kv_rank) // dt_cols
            kpe_idx = (q_rank + kv_rank + dt_cols) // LANES
            p1 = _mm(u, w_in[:, :s0].astype(BF16), BF16, tm_wide, _tile(s0, 512),
                     single_buffer_x=True)
            p2 = _mm(u, w_p2, F32, tm_big, _tile(w_p2.shape[1], 256))
            xbc = _conv_silu(p1, d_inner, conv_dim, ssd_conv_w[e], ssd_conv_b[e], n_lat, seq, ctx_len,
                             min(ctx_len, 256), _tile(int(np.gcd(conv_dim, d_inner)), 2048))
            y2 = _ssd_scan(xbc, p2, dt_idx, ssd_dt_bias[e], ssd_a_log[e], batch, seq, ctx_len, d_inner)
            ssd_out = _ssd_finish(y2, xbc, p1, ssd_d[e], ssd_norm[e], d_inner, tr)

            qn = _norm_rows(p2, q_rank, 0, mla_q_norm[e], BF16, m, tr)
            kvn = _norm_rows(p2, kv_rank, kv_idx, mla_kv_norm[e], BF16, m, tr)
            wq = mla_w_q_up[e].reshape(q_rank, MLA_HEADS, MLA_NOPE + MLA_ROPE)
            wq_rope = wq[:, :, MLA_NOPE:]
            wq3 = jnp.concatenate([wq[:, :, :MLA_NOPE], wq_rope, wq_rope[:, :, perm]], axis=2)
            wq3 = wq3.reshape(q_rank, MLA_HEADS * 2 * LANES).astype(BF16)
            qh = _mm_qrope(qn, wq3, cos_t, sin_t, tm_big, min(4, MLA_HEADS), rope_table_idx)
            kv = _mm(kvn, mla_w_kv_up[e].astype(BF16), BF16, tm_big, _tile(MLA_HEADS * 2 * LANES, 2048))
            attn = _mla_attention(qh, kv, p2, kpe_idx, cos_t, sin_t, batch, seq, ctx_len, min(seq, MLA_Q_TILE))
            h = _mm_res([ssd_out, attn], hyb_w_out_b, e, h, mods, 2, seg, rows, tm_big, _tile(d_model, 512),
                        single_buffer_x=True)
        else:
            o = layer // 2
            qkv = _mm(u, na_w_qkv, BF16, tm_wide, _tile(d_model, 512), layer=o, single_buffer_x=True,
                      scaled_cols=d_model, out_scale=float((d_model // NA_HEADS) ** -0.5 * LOG2_E))
            att = _na_attention(qkv, na_rpb[o], batch, seq, ctx_len, d_model, ctx_out)
            h = _mm_res([att], na_w_out_b, o, h, mods, 2, seg, rows, tm_big, _tile(d_model, 512))
        u2 = _norm_rows(h, d_model, 0, norm_ffn[layer], BF16, rows, tr, mods, (3, 4), seg)
        hidden = _mm_swiglu(u2, ffn_w_gate, ffn_w_up, layer, rows, _row_tile(rows, WIDE_ROW_TILE),
                            _tile(ffn_w_gate.shape[-1], 256))
        h = _mm_res([hidden], ffn_w_down_b, layer, h, mods, 5, seg, rows, tm_big,
                    _tile(d_model, 256), single_buffer_x=True)
    out = _norm_rows(h, d_model, 0, final_norm, F32, n_lat, tr)
    return out.reshape(batch, seq, d_model)
```

```python
import functools

import numpy as np
import jax
import jax.numpy as jnp
from jax import lax
from jax.experimental import pallas as pl
from jax.experimental.pallas import tpu as pltpu

F32 = jnp.float32
BF16 = jnp.bfloat16

NORM_EPS = 1e-6
N_MOD = 6
GRID_W = 64
ROPE_THETA = 10000.0
SSD_HEAD_DIM = 64
SSD_GROUPS = 8
SSD_STATE = 128
SSD_CONV = 5
SSD_CHUNK = 128
MLA_HEADS = 32
MLA_NOPE = 128
MLA_ROPE = 64
MLA_V = 128
NA_HEADS = 32
NA_WIN_H = 8
NA_WIN_W = 16
NA_Q_ROWS = 8
NA_K_ROWS = 16
NA_BLOCKS_PER_STEP = 4
NA_KEY_CHUNK = 256
MASK_NEG = -1e30
LOG2_E = 1.4426950408889634
MLA_KEY_CHUNK = 256
MLA_Q_TILE = 2048
LANES = 128
BF16_SUBLANES = 16
VMEM_LIMIT_BYTES = 56 * 2 ** 20
COND_ROWS = 16
WIDE_ROW_TILE = 2304


def _cp(*sem):
    return pltpu.CompilerParams(dimension_semantics=sem, vmem_limit_bytes=VMEM_LIMIT_BYTES)


def _tile(n, pref):
    if n <= pref:
        return n
    t = (pref // LANES) * LANES
    while t >= LANES:
        if n % t == 0:
            return t
        t -= LANES
    raise ValueError(f"no lane-aligned tile for {n}")


def _silu(v):
    return v * jax.nn.sigmoid(v)


def _ada_kernel(c_ref, w1_ref, w2_ref, b_ref, o_ref):
    cnd = c_ref[...]
    t = jnp.dot(_silu(cnd).astype(BF16), w1_ref[0].astype(BF16), preferred_element_type=F32)
    o_ref[0] = jnp.dot(t.astype(BF16), w2_ref[0].astype(BF16), preferred_element_type=F32) + b_ref[0]


def _ada(cond, w1, w2, bias):
    depth, d, r = w1.shape
    n = w2.shape[-1]
    tn = _tile(n, 2048)
    return pl.pallas_call(
        _ada_kernel,
        grid=(depth, n // tn),
        in_specs=[pl.BlockSpec((COND_ROWS, d), lambda l, j: (0, 0)),
                  pl.BlockSpec((1, d, r), lambda l, j: (l, 0, 0)),
                  pl.BlockSpec((1, r, tn), lambda l, j: (l, 0, j)),
                  pl.BlockSpec((1, 1, tn), lambda l, j: (l, 0, j))],
        out_specs=pl.BlockSpec((1, COND_ROWS, tn), lambda l, j: (l, 0, j)),
        out_shape=jax.ShapeDtypeStruct((depth, COND_ROWS, n), F32),
        compiler_params=_cp("arbitrary", "arbitrary"),
        name="ada",
    )(cond, w1, w2, bias.reshape(depth, 1, n))


def _norm_kernel(x_ref, g_ref, *rest, modulated):
    o_ref = rest[-1]
    xv = x_ref[...].astype(F32)
    y = xv * lax.rsqrt(jnp.mean(xv * xv, axis=-1, keepdims=True) + NORM_EPS) * g_ref[...]
    if modulated:
        sh_ref, sc_ref = rest[0], rest[1]
        y = y * (1.0 + sc_ref[0]) + sh_ref[0]
    o_ref[...] = y.astype(o_ref.dtype)


def _norm_rows(xa, width, col_idx, gain, out_dtype, n_rows, tr, mods=None, which=None, seg=None):
    in_specs = [pl.BlockSpec((tr, width), lambda i: (i, col_idx)),
                pl.BlockSpec((1, width), lambda i: (0, 0))]
    args = [xa, gain.reshape(1, width).astype(F32)]
    if mods is not None:
        sh_w, sc_w = which
        in_specs += [pl.BlockSpec((1, 1, width), lambda i: (seg(i * tr) * N_MOD + sh_w, 0, 0)),
                     pl.BlockSpec((1, 1, width), lambda i: (seg(i * tr) * N_MOD + sc_w, 0, 0))]
        args += [mods, mods]
    return pl.pallas_call(
        functools.partial(_norm_kernel, modulated=mods is not None),
        grid=(n_rows // tr,),
        in_specs=in_specs,
        out_specs=pl.BlockSpec((tr, width), lambda i: (i, 0)),
        out_shape=jax.ShapeDtypeStruct((n_rows, width), out_dtype),
        compiler_params=_cp("parallel"),
        name="norm",
    )(*args)


def _x_spec(tm, k, single_buffer):
    if single_buffer:
        return pl.BlockSpec((tm, k), lambda i, j: (i, 0), pipeline_mode=pl.Buffered(1))
    return pl.BlockSpec((tm, k), lambda i, j: (i, 0))


def _row_tile(rows, pref):
    t = (min(pref, rows) // BF16_SUBLANES) * BF16_SUBLANES
    while rows % t:
        t -= BF16_SUBLANES
    return t


def _w_spec(w, layer, k, tn):
    if w.ndim == 3:
        return pl.BlockSpec((None, k, tn), lambda i, j: (layer, 0, j))
    return pl.BlockSpec((k, tn), lambda i, j: (0, j))


def _mm_kernel(x_ref, w_ref, o_ref, *, scaled_tiles, out_scale):
    acc = jnp.dot(x_ref[...], w_ref[...].astype(BF16), preferred_element_type=F32)
    if scaled_tiles:
        acc = acc * jnp.where(pl.program_id(1) < scaled_tiles, out_scale, 1.0)
    o_ref[...] = acc.astype(o_ref.dtype)


def _mm(xa, w, out_dtype, tm, tn, layer=None, n=None, scaled_cols=0, out_scale=1.0, single_buffer_x=False):
    m, k = xa.shape
    n = w.shape[-1] if n is None else n
    return pl.pallas_call(
        functools.partial(_mm_kernel, scaled_tiles=scaled_cols // tn, out_scale=out_scale),
        grid=(m // tm, n // tn),
        in_specs=[_x_spec(tm, k, single_buffer_x), _w_spec(w, layer, k, tn)],
        out_specs=pl.BlockSpec((tm, tn), lambda i, j: (i, j)),
        out_shape=jax.ShapeDtypeStruct((m, n), out_dtype),
        compiler_params=_cp("parallel", "arbitrary"),
        name="mm",
    )(xa, w)


def _mm_swiglu_kernel(x_ref, wg_ref, wu_ref, o_ref):
    xv = x_ref[...]
    gate = jnp.dot(xv, wg_ref[...].astype(BF16), preferred_element_type=F32)
    up = jnp.dot(xv, wu_ref[...].astype(BF16), preferred_element_type=F32)
    o_ref[...] = (_silu(gate) * up).astype(o_ref.dtype)


def _mm_swiglu(xa, wg, wu, layer, rows, tm, tn):
    m, k = xa.shape
    n = wg.shape[-1]
    return pl.pallas_call(
        _mm_swiglu_kernel,
        grid=(rows // tm, n // tn),
        in_specs=[_x_spec(tm, k, True), _w_spec(wg, layer, k, tn), _w_spec(wu, layer, k, tn)],
        out_specs=pl.BlockSpec((tm, tn), lambda i, j: (i, j)),
        out_shape=jax.ShapeDtypeStruct((m, n), BF16),
        compiler_params=_cp("parallel", "arbitrary"),
        name="mm_swiglu",
    )(xa, wg, wu)


def _mm_res_kernel(*refs, n_pairs):
    o_ref = refs[-1]
    res_ref, gate_ref = refs[2 * n_pairs], refs[2 * n_pairs + 1]
    acc = None
    for i in range(n_pairs):
        part = jnp.dot(refs[i][...], refs[n_pairs + i][...], preferred_element_type=F32)
        acc = part if acc is None else acc + part
    o_ref[...] = res_ref[...] + gate_ref[0] * acc


def _mm_res(xas, w, layer, res, mods, which, seg, rows, tm, tn, single_buffer_x=False):
    m, n = res.shape
    n_pairs = len(xas)
    k = xas[0].shape[1]
    in_specs = [_x_spec(tm, k, single_buffer_x) for _ in xas]
    in_specs += [pl.BlockSpec((None, k, tn), functools.partial(lambda i, j, p: (layer, p, j), p=p))
                 for p in range(n_pairs)]
    in_specs += [pl.BlockSpec((tm, tn), lambda i, j: (i, j)),
                 pl.BlockSpec((1, 1, tn), lambda i, j: (seg(i * tm) * N_MOD + which, 0, j))]
    return pl.pallas_call(
        functools.partial(_mm_res_kernel, n_pairs=n_pairs),
        grid=(rows // tm, n // tn),
        in_specs=in_specs,
        out_specs=pl.BlockSpec((tm, tn), lambda i, j: (i, j)),
        out_shape=jax.ShapeDtypeStruct((m, n), F32),
        compiler_params=_cp("parallel", "arbitrary"),
        name="mm_res",
    )(*xas, *([w] * n_pairs), res, mods)


def _mm_qrope_kernel(x_ref, w_ref, cos_ref, sin_ref, o_ref, *, heads, out_scale):
    acc = jnp.dot(x_ref[...], w_ref[...], preferred_element_type=F32) * out_scale
    cs, sn = cos_ref[...], sin_ref[...]
    for hh in range(heads):
        base = hh * 2 * LANES
        o_ref[:, base:base + LANES] = acc[:, base:base + LANES].astype(o_ref.dtype)
        half = acc[:, base + LANES:base + 2 * LANES]
        rot = half * cs + pltpu.roll(half, MLA_ROPE, 1) * sn
        o_ref[:, base + LANES:base + 2 * LANES] = rot.astype(o_ref.dtype)


def _mm_qrope(xa, w, cos_t, sin_t, tm, heads_per_step, table_idx):
    m, k = xa.shape
    n = w.shape[1]
    tn = heads_per_step * 2 * LANES
    out_scale = float((MLA_NOPE + MLA_ROPE) ** -0.5 * LOG2_E)
    return pl.pallas_call(
        functools.partial(_mm_qrope_kernel, heads=heads_per_step, out_scale=out_scale),
        grid=(m // tm, n // tn),
        in_specs=[pl.BlockSpec((tm, k), lambda i, j: (i, 0)),
                  pl.BlockSpec((k, tn), lambda i, j: (0, j)),
                  pl.BlockSpec((tm, LANES), lambda i, j: (table_idx(i), 0)),
                  pl.BlockSpec((tm, LANES), lambda i, j: (table_idx(i), 0))],
        out_specs=pl.BlockSpec((tm, tn), lambda i, j: (i, j)),
        out_shape=jax.ShapeDtypeStruct((m, n), BF16),
        compiler_params=_cp("parallel", "arbitrary"),
        name="mm_qrope",
    )(xa, w, cos_t, sin_t)


def _conv_kernel(prev_ref, cur_ref, next_ref, w_ref, b_ref, o_ref, *, n_lat_blocks, blocks_per_seq, blocks_per_ctx):
    p = pl.program_id(0)
    is_ctx = p >= n_lat_blocks
    q = jnp.where(is_ctx, (p - n_lat_blocks) % blocks_per_ctx, p % blocks_per_seq)
    q_last = jnp.where(is_ctx, blocks_per_ctx - 1, blocks_per_seq - 1)
    keep_prev = jnp.where(q == 0, 0.0, 1.0)
    keep_next = jnp.where(q == q_last, 0.0, 1.0)
    xc = cur_ref[...].astype(F32)
    rows = xc.shape[0]
    halo = prev_ref.shape[0]
    full = jnp.concatenate([prev_ref[...].astype(F32) * keep_prev, xc,
                            next_ref[...].astype(F32) * keep_next], axis=0)
    total = rows + 2 * halo
    wv = w_ref[...]
    acc = jnp.zeros_like(xc) + b_ref[...]
    for tap in range(SSD_CONV):
        shift = (SSD_CONV // 2 - tap) % total
        moved = full if shift == 0 else pltpu.roll(full, shift, 0)
        acc = acc + wv[tap:tap + 1, :] * moved[halo:halo + rows]
    o_ref[...] = _silu(acc).astype(o_ref.dtype)


def _conv_silu(p1, col0, width, conv_w, conv_b, n_lat_rows, seq, ctx_len, rb, tc):
    m = p1.shape[0]
    halo = BF16_SUBLANES
    cb0 = col0 // tc
    per = rb // halo
    last_halo = m // halo - 1
    return pl.pallas_call(
        functools.partial(_conv_kernel, n_lat_blocks=n_lat_rows // rb, blocks_per_seq=seq // rb,
                          blocks_per_ctx=ctx_len // rb),
        grid=(m // rb, width // tc),
        in_specs=[pl.BlockSpec((halo, tc), lambda p, j: (jnp.maximum(p * per - 1, 0), cb0 + j)),
                  pl.BlockSpec((rb, tc), lambda p, j: (p, cb0 + j)),
                  pl.BlockSpec((halo, tc), lambda p, j: (jnp.minimum((p + 1) * per, last_halo), cb0 + j)),
                  pl.BlockSpec((SSD_CONV, tc), lambda p, j: (0, j)),
                  pl.BlockSpec((1, tc), lambda p, j: (0, j))],
        out_specs=pl.BlockSpec((rb, tc), lambda p, j: (p, j)),
        out_shape=jax.ShapeDtypeStruct((m, width), BF16),
        compiler_params=_cp("parallel", "arbitrary"),
        name="conv_silu",
    )(p1, p1, p1, jnp.transpose(conv_w).astype(F32), conv_b.reshape(1, width).astype(F32))


def _split3(v):
    hi = v.astype(BF16)
    r1 = v - hi.astype(F32)
    mid = r1.astype(BF16)
    lo = (r1 - mid.astype(F32)).astype(BF16)
    return hi, mid, lo


def _dot_exact_rhs(sel, v):
    return sum(jnp.dot(sel, part, preferred_element_type=F32) for part in _split3(v))


def _dot_split_lhs(v, sel_stack, terms):
    lhs = jnp.concatenate(_split3(v)[:terms], axis=1)
    return jnp.dot(lhs, sel_stack, preferred_element_type=F32)


def _ssd_kernel(xs_ref, b_ref, c_ref, dt_ref, bias_ref, alog_ref, y_ref, st_ref, ex_ref, *, heads,
                heads_per_group):
    d = pl.program_id(1)
    t = xs_ref.shape[0]
    hd = SSD_HEAD_DIM
    n = SSD_STATE
    width = heads_per_group * hd
    ncol = dt_ref.shape[1]

    @pl.when(pl.program_id(2) == 0)
    def _():
        st_ref[...] = jnp.zeros_like(st_ref)
        ri = lax.broadcasted_iota(jnp.int32, ex_ref.shape, 0) % ncol
        ci = lax.broadcasted_iota(jnp.int32, ex_ref.shape, 1)
        ex_ref[...] = (ri == d * heads + ci // hd).astype(BF16)

    xv = dt_ref[...] + bias_ref[...]
    dtv = jnp.maximum(xv, 0.0) + jnp.log1p(jnp.exp(-jnp.abs(xv)))
    da = dtv * (-jnp.exp(alog_ref[...]))
    qi = lax.broadcasted_iota(jnp.int32, (t, t), 0)
    si = lax.broadcasted_iota(jnp.int32, (t, t), 1)
    mask = (si - qi) * jnp.where(d == 0, 1, -1) <= 0
    acum = _dot_exact_rhs(mask.astype(BF16), da)

    acum_x = _dot_split_lhs(acum, ex_ref[...], 3)
    dtv_x = _dot_split_lhs(dtv, ex_ref[0:2 * ncol, :], 2)
    acum_t_all = jnp.transpose(acum)
    acum_t = jnp.where(d == 0, acum_t_all[0:heads], acum_t_all[heads:2 * heads])
    a_end = jnp.where(d == 0, acum_x[t - 1:t, :], acum_x[0:1, :])

    xdt = xs_ref[...].astype(F32) * dtv_x
    xdt_b = xdt.astype(BF16)
    scale_q = jnp.exp(acum_x)
    xw = (xdt * jnp.exp(a_end - acum_x)).astype(BF16)
    decay_end = jnp.exp(a_end)

    lane = lax.broadcasted_iota(jnp.int32, (t, LANES), 1)
    for g in range(SSD_GROUPS):
        gs = slice(g * width, (g + 1) * width)
        cg = c_ref[:, g * n:(g + 1) * n]
        bg = b_ref[:, g * n:(g + 1) * n]
        cb = lax.dot_general(cg, bg, (((1,), (1,)), ((), ())), preferred_element_type=F32)
        state = st_ref[g]
        y_state = jnp.dot(cg, state.astype(BF16), preferred_element_type=F32)
        bg_t = jnp.transpose(bg.astype(F32)).astype(BF16)
        st_ref[g] = state * decay_end[:, gs] + jnp.dot(bg_t, xw[:, gs], preferred_element_type=F32)
        for p in range(heads_per_group // 2):
            sl = slice(g * width + p * LANES, g * width + (p + 1) * LANES)
            mats = []
            for hh in (g * heads_per_group + 2 * p, g * heads_per_group + 2 * p + 1):
                a_q = acum_x[:, hh * hd:hh * hd + 1]
                a_s = acum_t[hh:hh + 1, :]
                mats.append(cb * jnp.exp(jnp.where(mask, a_q - a_s, -jnp.inf)))
            lcat = jnp.concatenate(mats, axis=1).astype(BF16)
            xp = xdt_b[:, sl]
            xcat = jnp.concatenate([jnp.where(lane < hd, xp, jnp.zeros_like(xp)),
                                    jnp.where(lane >= hd, xp, jnp.zeros_like(xp))], axis=0)
            y_in = jnp.dot(lcat, xcat, preferred_element_type=F32)
            y_ref[0, :, sl] = (y_in + y_state[:, p * LANES:(p + 1) * LANES] * scale_q[:, sl]).astype(y_ref.dtype)


def _ssd_scan(xbc, p2, dt_col_idx, dt_bias, a_log, batch, seq, ctx_len, d_inner):
    m = xbc.shape[0]
    t = SSD_CHUNK
    heads = d_inner // SSD_HEAD_DIM
    hpg = heads // SSD_GROUPS
    width = hpg * SSD_HEAD_DIM
    n = SSD_STATE
    nc_ctx, nc_lat = ctx_len // t, seq // t
    nch = nc_ctx + nc_lat
    ctx_base = batch * nc_lat
    ncol = _round_up(2 * heads, LANES)

    def row_block(b, d, i):
        ctx_blk = ctx_base + b * nc_ctx + jnp.where(d == 0, i, nc_ctx - 1 - i)
        lat_blk = b * nc_lat + jnp.where(d == 0, i - nc_ctx, nch - 1 - i)
        return jnp.where(i < nc_ctx, ctx_blk, lat_blk)

    gn = SSD_GROUPS * n
    b_col = d_inner // gn
    pad = ncol - 2 * heads
    bias = jnp.pad(dt_bias.reshape(1, 2 * heads).astype(F32), ((0, 0), (0, pad)))
    alog = jnp.pad(a_log.reshape(1, 2 * heads).astype(F32), ((0, 0), (0, pad)))
    return pl.pallas_call(
        functools.partial(_ssd_kernel, heads=heads, heads_per_group=hpg),
        grid=(batch, 2, nch),
        in_specs=[pl.BlockSpec((t, d_inner), lambda b, d, i: (row_block(b, d, i), 0)),
                  pl.BlockSpec((t, gn), lambda b, d, i: (row_block(b, d, i), b_col)),
                  pl.BlockSpec((t, gn), lambda b, d, i: (row_block(b, d, i), b_col + 1)),
                  pl.BlockSpec((t, ncol), lambda b, d, i: (row_block(b, d, i), dt_col_idx)),
                  pl.BlockSpec((1, ncol), lambda b, d, i: (0, 0)),
                  pl.BlockSpec((1, ncol), lambda b, d, i: (0, 0))],
        out_specs=pl.BlockSpec((1, t, d_inner), lambda b, d, i: (d, row_block(b, d, i), 0)),
        out_shape=jax.ShapeDtypeStruct((2, m, d_inner), BF16),
        scratch_shapes=[pltpu.VMEM((SSD_GROUPS, n, width), F32), pltpu.VMEM((3 * ncol, d_inner), BF16)],
        compiler_params=_cp("parallel", "parallel", "arbitrary"),
        name="ssd_scan",
    )(xbc, xbc, xbc, p2, bias, alog)


def _round_up(v, mult):
    return -(-v // mult) * mult


def _ssd_finish_kernel(y_ref, xs_ref, z_ref, d_ref, g_ref, o_ref):
    y = y_ref[0].astype(F32) + y_ref[1].astype(F32) + d_ref[...] * xs_ref[...].astype(F32)
    zv = z_ref[...].astype(F32)
    y = y * _silu(zv)
    gw = y.shape[1] // SSD_GROUPS
    for gi in range(SSD_GROUPS):
        sl = slice(gi * gw, (gi + 1) * gw)
        yg = y[:, sl]
        inv = lax.rsqrt(jnp.mean(yg * yg, axis=-1, keepdims=True) + NORM_EPS)
        o_ref[:, sl] = (yg * inv * g_ref[:, sl]).astype(o_ref.dtype)


def _ssd_finish(y2, xbc, p1, d_skip, ssd_norm, d_inner, tr):
    m = xbc.shape[0]
    d_x = jnp.repeat(d_skip.astype(F32), SSD_HEAD_DIM).reshape(1, d_inner)
    return pl.pallas_call(
        _ssd_finish_kernel,
        grid=(m // tr,),
        in_specs=[pl.BlockSpec((2, tr, d_inner), lambda i: (0, i, 0)),
                  pl.BlockSpec((tr, d_inner), lambda i: (i, 0)),
                  pl.BlockSpec((tr, d_inner), lambda i: (i, 0)),
                  pl.BlockSpec((1, d_inner), lambda i: (0, 0)),
                  pl.BlockSpec((1, d_inner), lambda i: (0, 0))],
        out_specs=pl.BlockSpec((tr, d_inner), lambda i: (i, 0)),
        out_shape=jax.ShapeDtypeStruct((m, d_inner), BF16),
        compiler_params=_cp("parallel"),
        name="ssd_finish",
    )(y2, xbc, p1, d_x, ssd_norm.reshape(1, d_inner).astype(F32))


def _mla_attn_kernel(*refs, ctx_len, with_latent, key_chunk):
    if with_latent:
        q_ref, kc_ref, vc_ref, pc_ref, kl_ref, vl_ref, pl_ref, cos_ref, sin_ref, o_ref, kt_ref, vs_ref = refs
        seq = kl_ref.shape[0]
    else:
        q_ref, kc_ref, vc_ref, pc_ref, _, o_ref, kt_ref, vs_ref = refs
        seq = 0

    @pl.when(pl.program_id(2) == 0)
    def _():
        lane = lax.broadcasted_iota(jnp.int32, (ctx_len, LANES), 1)
        kt_ref[0:LANES, 0:ctx_len] = jnp.transpose(kc_ref[...].astype(F32)).astype(BF16)
        kt_ref[LANES:, 0:ctx_len] = jnp.transpose(jnp.where(lane < MLA_ROPE, pc_ref[...], 0.0)).astype(BF16)
        vs_ref[0:ctx_len, :] = vc_ref[...]
        for a in range(0, seq, key_chunk):
            rows = slice(a, a + key_chunk)
            cols = slice(ctx_len + a, ctx_len + a + key_chunk)
            kt_ref[0:LANES, cols] = jnp.transpose(kl_ref[rows, :].astype(F32)).astype(BF16)
            blk = pl_ref[rows, :]
            rot = blk * cos_ref[rows, :] + pltpu.roll(blk, MLA_ROPE, 1) * sin_ref[rows, :]
            kt_ref[LANES:, cols] = jnp.transpose(rot).astype(BF16)
            vs_ref[cols, :] = vl_ref[rows, :]

    qv = q_ref[...]
    bounds = [(0, ctx_len)] + [(ctx_len + a, ctx_len + a + key_chunk) for a in range(0, seq, key_chunk)]
    m_run = acc = l_part = None
    for lo, hi in bounds:
        s = jnp.dot(qv, kt_ref[:, lo:hi], preferred_element_type=F32)
        m_new = jnp.max(s, axis=-1, keepdims=True)
        if m_run is not None:
            m_new = jnp.maximum(m_run, m_new)
        p = jnp.exp2(s - m_new)
        l_new = p[:, 0:LANES]
        for c in range(LANES, hi - lo, LANES):
            l_new = l_new + p[:, c:c + LANES]
        pv = jnp.dot(p.astype(BF16), vs_ref[lo:hi, :], preferred_element_type=F32)
        if m_run is None:
            acc, l_part = pv, l_new
        else:
            alpha = jnp.exp2(m_run - m_new)
            acc = alpha * acc + pv
            l_part = alpha * l_part + l_new
        m_run = m_new
    o_ref[...] = (acc / jnp.sum(l_part, axis=-1, keepdims=True)).astype(o_ref.dtype)


def _mla_attention(qh, kv, p2, kpe_idx, cos_t, sin_t, batch, seq, ctx_len, tq):
    m = qh.shape[0]
    heads = MLA_HEADS
    key_chunk = min(seq, MLA_KEY_CHUNK)
    cb = (batch * seq) // ctx_len
    nq = seq // tq
    out_shape = jax.ShapeDtypeStruct((m, heads * MLA_V), BF16)
    ctx_specs = [pl.BlockSpec((ctx_len, LANES), lambda b, h, i: (cb + b, 2 * h)),
                 pl.BlockSpec((ctx_len, LANES), lambda b, h, i: (cb + b, 2 * h + 1)),
                 pl.BlockSpec((ctx_len, LANES), lambda b, h, i: (cb + b, kpe_idx))]
    lat = pl.pallas_call(
        functools.partial(_mla_attn_kernel, ctx_len=ctx_len, with_latent=True, key_chunk=key_chunk),
        grid=(batch, heads, nq),
        in_specs=[pl.BlockSpec((tq, 2 * LANES), lambda b, h, i: (b * nq + i, h))] + ctx_specs + [
            pl.BlockSpec((seq, LANES), lambda b, h, i: (b, 2 * h)),
            pl.BlockSpec((seq, LANES), lambda b, h, i: (b, 2 * h + 1)),
            pl.BlockSpec((seq, LANES), lambda b, h, i: (b, kpe_idx)),
            pl.BlockSpec((seq, LANES), lambda b, h, i: (0, 0)),
            pl.BlockSpec((seq, LANES), lambda b, h, i: (0, 0))],
        out_specs=pl.BlockSpec((tq, MLA_V), lambda b, h, i: (b * nq + i, h)),
        out_shape=out_shape,
        scratch_shapes=[pltpu.VMEM((2 * LANES, ctx_len + seq), BF16),
                        pltpu.VMEM((ctx_len + seq, MLA_V), BF16)],
        compiler_params=_cp("parallel", "parallel", "arbitrary"),
        name="mla_attn_latent",
    )(qh, kv, kv, p2, kv, kv, p2, cos_t, sin_t)
    return pl.pallas_call(
        functools.partial(_mla_attn_kernel, ctx_len=ctx_len, with_latent=False, key_chunk=key_chunk),
        grid=(batch, heads, 1),
        in_specs=[pl.BlockSpec((ctx_len, 2 * LANES), lambda b, h, i: (cb + b, h))] + ctx_specs + [
            pl.BlockSpec(memory_space=pl.ANY)],
        out_specs=pl.BlockSpec((ctx_len, MLA_V), lambda b, h, i: (cb + b, h)),
        out_shape=out_shape,
        scratch_shapes=[pltpu.VMEM((2 * LANES, ctx_len), BF16), pltpu.VMEM((ctx_len, MLA_V), BF16)],
        input_output_aliases={4: 0},
        compiler_params=_cp("parallel", "parallel", "arbitrary"),
        name="mla_attn_ctx",
    )(qh, kv, kv, p2, lat)


def _na_kernel(q_ref, k_ref, v_ref, kc_ref, vc_ref, pair_ref, o_ref, tab_ref, *, grid_rows, n_blk,
               blocks_per_step, key_chunk, plan):
    step = pl.program_id(2)

    @pl.when(jnp.logical_and(pl.program_id(1) == 0, step == 0))
    def _():
        for case, per_row in enumerate(plan):
            for qa, per_pair in enumerate(per_row):
                for i, src in enumerate(per_pair):
                    if src < 0:
                        blk = jnp.full((GRID_W, 2 * GRID_W), MASK_NEG, F32)
                    else:
                        blk = pair_ref[0, src] * LOG2_E
                    tab_ref[case, qa * GRID_W:(qa + 1) * GRID_W, i * 2 * GRID_W:(i + 1) * 2 * GRID_W] = blk

    nt = (((1,), (1,)), ((), ()))
    tq = NA_Q_ROWS * GRID_W
    n_keys = NA_K_ROWS * GRID_W
    for sub in range(blocks_per_step):
        j = step * blocks_per_step + sub
        case = jnp.where(j == 0, 0, jnp.where(j == n_blk - 1, 2, 1))
        ws = jnp.clip(NA_Q_ROWS * j - NA_WIN_H // 2, 0, grid_rows - NA_K_ROWS)
        rows = pl.ds(sub * tq, tq)
        qv = q_ref[rows, :]
        m_run = acc = l_part = None
        for c0 in [None] + list(range(0, n_keys, key_chunk)):
            if c0 is None:
                s = lax.dot_general(qv, kc_ref[...], nt, preferred_element_type=F32)
                vv = vc_ref[...]
            else:
                krows = pl.ds(pl.multiple_of(ws * GRID_W + c0, GRID_W), key_chunk)
                s = (lax.dot_general(qv, k_ref[krows, :], nt, preferred_element_type=F32)
                     + tab_ref[case, :, c0:c0 + key_chunk])
                vv = v_ref[krows, :]
            m_new = jnp.max(s, axis=-1, keepdims=True)
            if m_run is not None:
                m_new = jnp.maximum(m_run, m_new)
            p = jnp.exp2(s - m_new)
            l_new = p[:, 0:LANES]
            for c in range(LANES, s.shape[1], LANES):
                l_new = l_new + p[:, c:c + LANES]
            pv = jnp.dot(p.astype(BF16), vv, preferred_element_type=F32)
            if m_run is None:
                acc, l_part = pv, l_new
            else:
                alpha = jnp.exp2(m_run - m_new)
                acc = alpha * acc + pv
                l_part = alpha * l_part + l_new
            m_run = m_new
        o_ref[rows, :] = (acc / jnp.sum(l_part, axis=-1, keepdims=True)).astype(o_ref.dtype)


def _attn_small_kernel(q_ref, k_ref, v_ref, _, o_ref):
    s = lax.dot_general(q_ref[...], k_ref[...], (((1,), (1,)), ((), ())), preferred_element_type=F32)
    p = jnp.exp2(s - jnp.max(s, axis=-1, keepdims=True))
    o = jnp.dot(p.astype(BF16), v_ref[...], preferred_element_type=F32)
    o_ref[...] = (o / jnp.sum(p, axis=-1, keepdims=True)).astype(o_ref.dtype)


def _na_pair_blocks(rpb):
    qc = np.arange(GRID_W)[:, None]
    kc = np.arange(GRID_W)[None, :]
    c0 = np.clip(qc - NA_WIN_W // 2, 0, GRID_W - NA_WIN_W)
    col_ok = (kc >= c0) & (kc < c0 + NA_WIN_W)
    dc = np.clip(kc - qc + NA_WIN_W - 1, 0, 2 * NA_WIN_W - 2)
    a = jnp.take(rpb.astype(F32), jnp.asarray(dc), axis=2)
    a = jnp.where(jnp.asarray(col_ok), a, MASK_NEG)
    neg = jnp.full_like(a, MASK_NEG)
    return jnp.concatenate([jnp.concatenate([a[:, :-1], a[:, 1:]], axis=-1),
                            jnp.concatenate([a, neg], axis=-1),
                            jnp.concatenate([neg, a], axis=-1)], axis=1)


def _na_table_plan(grid_rows):
    n_blk = grid_rows // NA_Q_ROWS
    n_dr = 2 * NA_WIN_H - 1

    def block_plan(j):
        ws = int(np.clip(NA_Q_ROWS * j - NA_WIN_H // 2, 0, grid_rows - NA_K_ROWS))
        rows = []
        for qa in range(NA_Q_ROWS):
            qr = NA_Q_ROWS * j + qa
            r0 = int(np.clip(qr - NA_WIN_H // 2, 0, grid_rows - NA_WIN_H))
            pairs = []
            for i in range(NA_K_ROWS // 2):
                kr0 = ws + 2 * i
                ok0 = r0 <= kr0 < r0 + NA_WIN_H
                ok1 = r0 <= kr0 + 1 < r0 + NA_WIN_H
                dr0 = kr0 - qr + NA_WIN_H - 1
                if ok0 and ok1:
                    pairs.append(dr0)
                elif ok0:
                    pairs.append(n_dr - 1 + dr0)
                elif ok1:
                    pairs.append(2 * n_dr - 1 + dr0 + 1)
                else:
                    pairs.append(-1)
            rows.append(tuple(pairs))
        return tuple(rows)

    plans = [block_plan(j) for j in range(n_blk)]
    cases = (plans[0], plans[min(1, n_blk - 1)], plans[n_blk - 1])
    for j in range(1, n_blk - 1):
        if plans[j] != cases[1]:
            raise ValueError("neighbourhood-attention middle blocks are not translation invariant")
    return cases


def _na_attention(qkv, rpb, batch, seq, ctx_len, d_model, ctx_out):
    m = qkv.shape[0]
    heads = NA_HEADS
    hd = d_model // heads
    grid_rows = seq // GRID_W
    n_blk = grid_rows // NA_Q_ROWS
    bps = NA_BLOCKS_PER_STEP if n_blk % NA_BLOCKS_PER_STEP == 0 else 1
    n_steps = n_blk // bps
    tq = bps * NA_Q_ROWS * GRID_W
    tk = NA_K_ROWS * GRID_W
    cb = (batch * seq) // ctx_len
    k0, v0 = d_model // hd, 2 * d_model // hd
    pairs = _na_pair_blocks(rpb)
    n_pairs = pairs.shape[1]
    out_shape = jax.ShapeDtypeStruct((m, d_model), BF16)

    lat = pl.pallas_call(
        functools.partial(_na_kernel, grid_rows=grid_rows, n_blk=n_blk, blocks_per_step=bps,
                          key_chunk=min(tk, NA_KEY_CHUNK), plan=_na_table_plan(grid_rows)),
        grid=(heads, batch, n_steps),
        in_specs=[pl.BlockSpec((tq, hd), lambda h, b, j: (b * n_steps + j, h)),
                  pl.BlockSpec((seq, hd), lambda h, b, j: (b, k0 + h)),
                  pl.BlockSpec((seq, hd), lambda h, b, j: (b, v0 + h)),
                  pl.BlockSpec((ctx_len, hd), lambda h, b, j: (cb + b, k0 + h)),
                  pl.BlockSpec((ctx_len, hd), lambda h, b, j: (cb + b, v0 + h)),
                  pl.BlockSpec((1, n_pairs, GRID_W, 2 * GRID_W), lambda h, b, j: (h, 0, 0, 0))],
        out_specs=pl.BlockSpec((tq, hd), lambda h, b, j: (b * n_steps + j, h)),
        out_shape=out_shape,
        scratch_shapes=[pltpu.VMEM((3, NA_Q_ROWS * GRID_W, tk), F32)],
        compiler_params=_cp("arbitrary", "arbitrary", "arbitrary"),
        name="na_attn_latent",
    )(qkv, qkv, qkv, qkv, qkv, pairs)
    if not ctx_out:
        return lat
    return pl.pallas_call(
        _attn_small_kernel,
        grid=(batch, heads),
        in_specs=[pl.BlockSpec((ctx_len, hd), lambda b, h: (cb + b, h)),
                  pl.BlockSpec((ctx_len, hd), lambda b, h: (cb + b, k0 + h)),
                  pl.BlockSpec((ctx_len, hd), lambda b, h: (cb + b, v0 + h)),
                  pl.BlockSpec(memory_space=pl.ANY)],
        out_specs=pl.BlockSpec((ctx_len, hd), lambda b, h: (cb + b, h)),
        out_shape=out_shape,
        input_output_aliases={3: 0},
        compiler_params=_cp("parallel", "parallel"),
        name="na_attn_ctx",
    )(qkv, qkv, qkv, lat)


def _rope_tables(seq, pad_rows):
    pos = np.arange(seq)
    n_freq = MLA_ROPE // 4
    inv_freq = jnp.power(ROPE_THETA, -jnp.arange(n_freq, dtype=F32) / n_freq)
    rows = jnp.asarray(pos // GRID_W, F32)[:, None] * inv_freq
    cols = jnp.asarray(pos % GRID_W, F32)[:, None] * inv_freq
    ang = jnp.concatenate([rows, rows, cols, cols], axis=-1)
    sign = np.tile(np.concatenate([-np.ones(n_freq), np.ones(n_freq)]), 2).astype(np.float32)
    cos_t = jnp.concatenate([jnp.cos(ang), jnp.zeros((seq, LANES - MLA_ROPE), F32)], axis=1)
    sin_t = jnp.concatenate([jnp.sin(ang) * sign, jnp.zeros((seq, LANES - MLA_ROPE), F32)], axis=1)
    ident = jnp.concatenate([jnp.ones((pad_rows, MLA_ROPE), F32), jnp.zeros((pad_rows, LANES - MLA_ROPE), F32)], 1)
    return (jnp.concatenate([cos_t, ident], axis=0),
            jnp.concatenate([sin_t, jnp.zeros((pad_rows, LANES), F32)], axis=0))


def _rope_partner_perm():
    n_freq = MLA_ROPE // 4
    dd = np.arange(MLA_ROPE)
    return np.where(dd % (2 * n_freq) < n_freq, dd + n_freq, dd - n_freq)


def kernel(x, c, ctx, c_ctx, ada_w1, ada_w2, ada_b, norm_mix, norm_ffn, ffn_w_gate, ffn_w_up, ffn_w_down,
           hyb_w_in, ssd_conv_w, ssd_conv_b, ssd_dt_bias, ssd_a_log, ssd_d, ssd_norm,
           mla_q_norm, mla_w_q_up, mla_kv_norm, mla_w_kv_up, hyb_w_out,
           na_w_qkv, na_rpb, na_w_out, final_norm):
    batch, seq, d_model = x.shape
    ctx_len = ctx.shape[1]
    depth = ada_w1.shape[0]
    n_lat = batch * seq
    m = n_lat + batch * ctx_len
    d_inner = ssd_norm.shape[-1]
    heads = d_inner // SSD_HEAD_DIM
    conv_dim = ssd_conv_w.shape[1]
    q_rank = mla_q_norm.shape[-1]
    kv_rank = mla_kv_norm.shape[-1]
    tseg = int(np.gcd(seq, batch * ctx_len))
    tseg = min(tseg, 1024)
    tm_big = tseg
    tm_wide = _row_tile(m, WIDE_ROW_TILE)
    tr = min(tseg, 256)

    def seg(row0):
        return jnp.where(row0 < n_lat, row0 // seq, batch)

    h = jnp.concatenate([x.reshape(n_lat, d_model), ctx.reshape(batch * ctx_len, d_model)], axis=0)
    cond = jnp.zeros((COND_ROWS, d_model), F32).at[:batch].set(c).at[batch].set(c_ctx)
    mods_all = _ada(cond, ada_w1, ada_w2, ada_b).reshape(depth, COND_ROWS * N_MOD, 1, d_model)

    dt_cols = _round_up(2 * heads, LANES)
    perm = _rope_partner_perm()
    cos_t, sin_t = _rope_tables(seq, tm_big)
    lat_tiles = n_lat // tm_big
    tiles_per_seq = seq // tm_big

    def rope_table_idx(i):
        return jnp.where(i < lat_tiles, i % tiles_per_seq, tiles_per_seq)

    ffn_w_down_b = ffn_w_down.astype(BF16)
    hyb_w_out_b = hyb_w_out.astype(BF16)
    na_w_out_b = na_w_out.astype(BF16)
    for layer in range(depth):
        ctx_out = layer < depth - 1
        rows = m if ctx_out else n_lat
        mods = mods_all[layer]
        u = _norm_rows(h, d_model, 0, norm_mix[layer], BF16, m, tr, mods, (0, 1), seg)
        if layer % 2 == 0:
            e = layer // 2
            w_in = hyb_w_in[e]
            s0 = d_inner + conv_dim
            s1 = s0 + 2 * heads
            s2 = s1 + q_rank
            s3 = s2 + kv_rank
            w_kpe = w_in[:, s3:]
            w_p2 = jnp.concatenate([w_in[:, s1:s2], w_in[:, s2:s3],
                                    jnp.pad(w_in[:, s0:s1], ((0, 0), (0, dt_cols - 2 * heads))),
                                    w_kpe, w_kpe[:, perm]], axis=1)
            kv_idx = q_rank // kv_rank
            dt_idx = (q_rank + kv_rank) // dt_cols
            kpe_idx = (q_rank + kv_rank + dt_cols) // LANES
            p1 = _mm(u, w_in[:, :s0].astype(BF16), BF16, tm_wide, _tile(s0, 512),
                     single_buffer_x=True)
            p2 = _mm(u, w_p2, F32, tm_big, _tile(w_p2.shape[1], 256))
            xbc = _conv_silu(p1, d_inner, conv_dim, ssd_conv_w[e], ssd_conv_b[e], n_lat, seq, ctx_len,
                             min(ctx_len, 256), _tile(int(np.gcd(conv_dim, d_inner)), 2048))
            y2 = _ssd_scan(xbc, p2, dt_idx, ssd_dt_bias[e], ssd_a_log[e], batch, seq, ctx_len, d_inner)
            ssd_out = _ssd_finish(y2, xbc, p1, ssd_d[e], ssd_norm[e], d_inner, tr)

            qn = _norm_rows(p2, q_rank, 0, mla_q_norm[e], BF16, m, tr)
            kvn = _norm_rows(p2, kv_rank, kv_idx, mla_kv_norm[e], BF16, m, tr)
            wq = mla_w_q_up[e].reshape(q_rank, MLA_HEADS, MLA_NOPE + MLA_ROPE)
            wq_rope = wq[:, :, MLA_NOPE:]
            wq3 = jnp.concatenate([wq[:, :, :MLA_NOPE], wq_rope, wq_rope[:, :, perm]], axis=2)
            wq3 = wq3.reshape(q_rank, MLA_HEADS * 2 * LANES).astype(BF16)
            qh = _mm_qrope(qn, wq3, cos_t, sin_t, tm_big, min(4, MLA_HEADS), rope_table_idx)
            kv = _mm(kvn, mla_w_kv_up[e].astype(BF16), BF16, tm_big, _tile(MLA_HEADS * 2 * LANES, 2048))
            attn = _mla_attention(qh, kv, p2, kpe_idx, cos_t, sin_t, batch, seq, ctx_len, min(seq, MLA_Q_TILE))
            h = _mm_res([ssd_out, attn], hyb_w_out_b, e, h, mods, 2, seg, rows, min(tseg, 512),
                        _tile(d_model, 512))
        else:
            o = layer // 2
            qkv = _mm(u, na_w_qkv, BF16, tm_wide, _tile(d_model, 512), layer=o, single_buffer_x=True,
                      scaled_cols=d_model, out_scale=float((d_model // NA_HEADS) ** -0.5 * LOG2_E))
            att = _na_attention(qkv, na_rpb[o], batch, seq, ctx_len, d_model, ctx_out)
            h = _mm_res([att], na_w_out_b, o, h, mods, 2, seg, rows, tm_big, _tile(d_model, 512))
        u2 = _norm_rows(h, d_model, 0, norm_ffn[layer], BF16, rows, tr, mods, (3, 4), seg)
        hidden = _mm_swiglu(u2, ffn_w_gate, ffn_w_up, layer, rows, _row_tile(rows, WIDE_ROW_TILE),
                            _tile(ffn_w_gate.shape[-1], 256))
        h = _mm_res([hidden], ffn_w_down_b, layer, h, mods, 5, seg, rows, tm_big,
                    _tile(d_model, 256), single_buffer_x=True)
    out = _norm_rows(h, d_model, 0, final_norm, F32, n_lat, tr)
    return out.reshape(batch, seq, d_model)
```

```python
import functools

import numpy as np
import jax
import jax.numpy as jnp
from jax import lax
from jax.experimental import pallas as pl
from jax.experimental.pallas import tpu as pltpu

F32 = jnp.float32
BF16 = jnp.bfloat16

NORM_EPS = 1e-6
N_MOD = 6
GRID_W = 64
ROPE_THETA = 10000.0
SSD_HEAD_DIM = 64
SSD_GROUPS = 8
SSD_STATE = 128
SSD_CONV = 5
SSD_CHUNK = 128
MLA_HEADS = 32
MLA_NOPE = 128
MLA_ROPE = 64
MLA_V = 128
NA_HEADS = 32
NA_WIN_H = 8
NA_WIN_W = 16
NA_Q_ROWS = 8
NA_K_ROWS = 16
NA_BLOCKS_PER_STEP = 4
NA_KEY_CHUNK = 256
MASK_NEG = -1e30
LOG2_E = 1.4426950408889634
MLA_KEY_CHUNK = 256
MLA_Q_TILE = 2048
LANES = 128
BF16_SUBLANES = 16
VMEM_LIMIT_BYTES = 56 * 2 ** 20
COND_ROWS = 16
WIDE_ROW_TILE = 2304


def _cp(*sem, vmem_limit=VMEM_LIMIT_BYTES):
    return pltpu.CompilerParams(dimension_semantics=sem, vmem_limit_bytes=vmem_limit)


def _tile(n, pref):
    if n <= pref:
        return n
    t = (pref // LANES) * LANES
    while t >= LANES:
        if n % t == 0:
            return t
        t -= LANES
    raise ValueError(f"no lane-aligned tile for {n}")


def _silu(v):
    return v * jax.nn.sigmoid(v)


def _ada_kernel(c_ref, w1_ref, w2_ref, b_ref, o_ref):
    cnd = c_ref[...]
    t = jnp.dot(_silu(cnd).astype(BF16), w1_ref[0].astype(BF16), preferred_element_type=F32)
    o_ref[0] = jnp.dot(t.astype(BF16), w2_ref[0].astype(BF16), preferred_element_type=F32) + b_ref[0]


def _ada(cond, w1, w2, bias):
    depth, d, r = w1.shape
    n = w2.shape[-1]
    tn = _tile(n, 2048)
    return pl.pallas_call(
        _ada_kernel,
        grid=(depth, n // tn),
        in_specs=[pl.BlockSpec((COND_ROWS, d), lambda l, j: (0, 0)),
                  pl.BlockSpec((1, d, r), lambda l, j: (l, 0, 0)),
                  pl.BlockSpec((1, r, tn), lambda l, j: (l, 0, j)),
                  pl.BlockSpec((1, 1, tn), lambda l, j: (l, 0, j))],
        out_specs=pl.BlockSpec((1, COND_ROWS, tn), lambda l, j: (l, 0, j)),
        out_shape=jax.ShapeDtypeStruct((depth, COND_ROWS, n), F32),
        compiler_params=_cp("arbitrary", "arbitrary"),
        name="ada",
    )(cond, w1, w2, bias.reshape(depth, 1, n))


def _norm_kernel(x_ref, g_ref, *rest, modulated):
    o_ref = rest[-1]
    xv = x_ref[...].astype(F32)
    y = xv * lax.rsqrt(jnp.mean(xv * xv, axis=-1, keepdims=True) + NORM_EPS) * g_ref[...]
    if modulated:
        sh_ref, sc_ref = rest[0], rest[1]
        y = y * (1.0 + sc_ref[0]) + sh_ref[0]
    o_ref[...] = y.astype(o_ref.dtype)


def _norm_rows(xa, width, col_idx, gain, out_dtype, n_rows, tr, mods=None, which=None, seg=None):
    in_specs = [pl.BlockSpec((tr, width), lambda i: (i, col_idx)),
                pl.BlockSpec((1, width), lambda i: (0, 0))]
    args = [xa, gain.reshape(1, width).astype(F32)]
    if mods is not None:
        sh_w, sc_w = which
        in_specs += [pl.BlockSpec((1, 1, width), lambda i: (seg(i * tr) * N_MOD + sh_w, 0, 0)),
                     pl.BlockSpec((1, 1, width), lambda i: (seg(i * tr) * N_MOD + sc_w, 0, 0))]
        args += [mods, mods]
    return pl.pallas_call(
        functools.partial(_norm_kernel, modulated=mods is not None),
        grid=(n_rows // tr,),
        in_specs=in_specs,
        out_specs=pl.BlockSpec((tr, width), lambda i: (i, 0)),
        out_shape=jax.ShapeDtypeStruct((n_rows, width), out_dtype),
        compiler_params=_cp("parallel"),
        name="norm",
    )(*args)


def _x_spec(tm, k, single_buffer):
    if single_buffer:
        return pl.BlockSpec((tm, k), lambda i, j: (i, 0), pipeline_mode=pl.Buffered(1))
    return pl.BlockSpec((tm, k), lambda i, j: (i, 0))


def _row_tile(rows, pref):
    t = (min(pref, rows) // BF16_SUBLANES) * BF16_SUBLANES
    while rows % t:
        t -= BF16_SUBLANES
    return t


def _w_spec(w, layer, k, tn):
    if w.ndim == 3:
        return pl.BlockSpec((None, k, tn), lambda i, j: (layer, 0, j))
    return pl.BlockSpec((k, tn), lambda i, j: (0, j))


def _mm_kernel(x_ref, w_ref, o_ref, *, scaled_tiles, out_scale):
    acc = jnp.dot(x_ref[...], w_ref[...].astype(BF16), preferred_element_type=F32)
    if scaled_tiles:
        acc = acc * jnp.where(pl.program_id(1) < scaled_tiles, out_scale, 1.0)
    o_ref[...] = acc.astype(o_ref.dtype)


def _mm(xa, w, out_dtype, tm, tn, layer=None, n=None, scaled_cols=0, out_scale=1.0, single_buffer_x=False):
    m, k = xa.shape
    n = w.shape[-1] if n is None else n
    return pl.pallas_call(
        functools.partial(_mm_kernel, scaled_tiles=scaled_cols // tn, out_scale=out_scale),
        grid=(m // tm, n // tn),
        in_specs=[_x_spec(tm, k, single_buffer_x), _w_spec(w, layer, k, tn)],
        out_specs=pl.BlockSpec((tm, tn), lambda i, j: (i, j)),
        out_shape=jax.ShapeDtypeStruct((m, n), out_dtype),
        compiler_params=_cp("parallel", "arbitrary"),
        name="mm",
    )(xa, w)


def _mm_swiglu_kernel(x_ref, wg_ref, wu_ref, o_ref):
    xv = x_ref[...]
    gate = jnp.dot(xv, wg_ref[...].astype(BF16), preferred_element_type=F32)
    up = jnp.dot(xv, wu_ref[...].astype(BF16), preferred_element_type=F32)
    o_ref[...] = (_silu(gate) * up).astype(o_ref.dtype)


def _mm_swiglu(xa, wg, wu, layer, rows, tm, tn):
    m, k = xa.shape
    n = wg.shape[-1]
    return pl.pallas_call(
        _mm_swiglu_kernel,
        grid=(rows // tm, n // tn),
        in_specs=[_x_spec(tm, k, True), _w_spec(wg, layer, k, tn), _w_spec(wu, layer, k, tn)],
        out_specs=pl.BlockSpec((tm, tn), lambda i, j: (i, j)),
        out_shape=jax.ShapeDtypeStruct((m, n), BF16),
        compiler_params=_cp("parallel", "arbitrary"),
        name="mm_swiglu",
    )(xa, wg, wu)


def _mm_res_kernel(*refs, n_pairs):
    o_ref = refs[-1]
    res_ref, gate_ref = refs[2 * n_pairs], refs[2 * n_pairs + 1]
    acc = None
    for i in range(n_pairs):
        part = jnp.dot(refs[i][...], refs[n_pairs + i][...], preferred_element_type=F32)
        acc = part if acc is None else acc + part
    o_ref[...] = res_ref[...] + gate_ref[0] * acc


def _mm_res(xas, w, layer, res, mods, which, seg, rows, tm, tn, single_buffer_x=False,
            vmem_limit=VMEM_LIMIT_BYTES):
    m, n = res.shape
    n_pairs = len(xas)
    k = xas[0].shape[1]
    in_specs = [_x_spec(tm, k, single_buffer_x) for _ in xas]
    in_specs += [pl.BlockSpec((None, k, tn), functools.partial(lambda i, j, p: (layer, p, j), p=p))
                 for p in range(n_pairs)]
    in_specs += [pl.BlockSpec((tm, tn), lambda i, j: (i, j)),
                 pl.BlockSpec((1, 1, tn), lambda i, j: (seg(i * tm) * N_MOD + which, 0, j))]
    return pl.pallas_call(
        functools.partial(_mm_res_kernel, n_pairs=n_pairs),
        grid=(rows // tm, n // tn),
        in_specs=in_specs,
        out_specs=pl.BlockSpec((tm, tn), lambda i, j: (i, j)),
        out_shape=jax.ShapeDtypeStruct((m, n), F32),
        compiler_params=_cp("parallel", "arbitrary", vmem_limit=vmem_limit),
        name="mm_res",
    )(*xas, *([w] * n_pairs), res, mods)


def _mm_qrope_kernel(x_ref, w_ref, cos_ref, sin_ref, o_ref, *, heads, out_scale):
    acc = jnp.dot(x_ref[...], w_ref[...], preferred_element_type=F32) * out_scale
    cs, sn = cos_ref[...], sin_ref[...]
    for hh in range(heads):
        base = hh * 2 * LANES
        o_ref[:, base:base + LANES] = acc[:, base:base + LANES].astype(o_ref.dtype)
        half = acc[:, base + LANES:base + 2 * LANES]
        rot = half * cs + pltpu.roll(half, MLA_ROPE, 1) * sn
        o_ref[:, base + LANES:base + 2 * LANES] = rot.astype(o_ref.dtype)


def _mm_qrope(xa, w, cos_t, sin_t, tm, heads_per_step, table_idx):
    m, k = xa.shape
    n = w.shape[1]
    tn = heads_per_step * 2 * LANES
    out_scale = float((MLA_NOPE + MLA_ROPE) ** -0.5 * LOG2_E)
    return pl.pallas_call(
        functools.partial(_mm_qrope_kernel, heads=heads_per_step, out_scale=out_scale),
        grid=(m // tm, n // tn),
        in_specs=[pl.BlockSpec((tm, k), lambda i, j: (i, 0)),
                  pl.BlockSpec((k, tn), lambda i, j: (0, j)),
                  pl.BlockSpec((tm, LANES), lambda i, j: (table_idx(i), 0)),
                  pl.BlockSpec((tm, LANES), lambda i, j: (table_idx(i), 0))],
        out_specs=pl.BlockSpec((tm, tn), lambda i, j: (i, j)),
        out_shape=jax.ShapeDtypeStruct((m, n), BF16),
        compiler_params=_cp("parallel", "arbitrary"),
        name="mm_qrope",
    )(xa, w, cos_t, sin_t)


def _conv_kernel(prev_ref, cur_ref, next_ref, w_ref, b_ref, o_ref, *, n_lat_blocks, blocks_per_seq, blocks_per_ctx):
    p = pl.program_id(0)
    is_ctx = p >= n_lat_blocks
    q = jnp.where(is_ctx, (p - n_lat_blocks) % blocks_per_ctx, p % blocks_per_seq)
    q_last = jnp.where(is_ctx, blocks_per_ctx - 1, blocks_per_seq - 1)
    keep_prev = jnp.where(q == 0, 0.0, 1.0)
    keep_next = jnp.where(q == q_last, 0.0, 1.0)
    xc = cur_ref[...].astype(F32)
    rows = xc.shape[0]
    halo = prev_ref.shape[0]
    full = jnp.concatenate([prev_ref[...].astype(F32) * keep_prev, xc,
                            next_ref[...].astype(F32) * keep_next], axis=0)
    total = rows + 2 * halo
    wv = w_ref[...]
    acc = jnp.zeros_like(xc) + b_ref[...]
    for tap in range(SSD_CONV):
        shift = (SSD_CONV // 2 - tap) % total
        moved = full if shift == 0 else pltpu.roll(full, shift, 0)
        acc = acc + wv[tap:tap + 1, :] * moved[halo:halo + rows]
    o_ref[...] = _silu(acc).astype(o_ref.dtype)


def _conv_silu(p1, col0, width, conv_w, conv_b, n_lat_rows, seq, ctx_len, rb, tc):
    m = p1.shape[0]
    halo = BF16_SUBLANES
    cb0 = col0 // tc
    per = rb // halo
    last_halo = m // halo - 1
    return pl.pallas_call(
        functools.partial(_conv_kernel, n_lat_blocks=n_lat_rows // rb, blocks_per_seq=seq // rb,
                          blocks_per_ctx=ctx_len // rb),
        grid=(m // rb, width // tc),
        in_specs=[pl.BlockSpec((halo, tc), lambda p, j: (jnp.maximum(p * per - 1, 0), cb0 + j)),
                  pl.BlockSpec((rb, tc), lambda p, j: (p, cb0 + j)),
                  pl.BlockSpec((halo, tc), lambda p, j: (jnp.minimum((p + 1) * per, last_halo), cb0 + j)),
                  pl.BlockSpec((SSD_CONV, tc), lambda p, j: (0, j)),
                  pl.BlockSpec((1, tc), lambda p, j: (0, j))],
        out_specs=pl.BlockSpec((rb, tc), lambda p, j: (p, j)),
        out_shape=jax.ShapeDtypeStruct((m, width), BF16),
        compiler_params=_cp("parallel", "arbitrary"),
        name="conv_silu",
    )(p1, p1, p1, jnp.transpose(conv_w).astype(F32), conv_b.reshape(1, width).astype(F32))


def _split3(v):
    hi = v.astype(BF16)
    r1 = v - hi.astype(F32)
    mid = r1.astype(BF16)
    lo = (r1 - mid.astype(F32)).astype(BF16)
    return hi, mid, lo


def _dot_exact_rhs(sel, v):
    return sum(jnp.dot(sel, part, preferred_element_type=F32) for part in _split3(v))


def _dot_split_lhs(v, sel_stack, terms):
    lhs = jnp.concatenate(_split3(v)[:terms], axis=1)
    return jnp.dot(lhs, sel_stack, preferred_element_type=F32)


def _ssd_kernel(xs_ref, b_ref, c_ref, dt_ref, bias_ref, alog_ref, y_ref, st_ref, ex_ref, *, heads,
                heads_per_group):
    d = pl.program_id(1)
    t = xs_ref.shape[0]
    hd = SSD_HEAD_DIM
    n = SSD_STATE
    width = heads_per_group * hd
    ncol = dt_ref.shape[1]

    @pl.when(pl.program_id(2) == 0)
    def _():
        st_ref[...] = jnp.zeros_like(st_ref)
        ri = lax.broadcasted_iota(jnp.int32, ex_ref.shape, 0) % ncol
        ci = lax.broadcasted_iota(jnp.int32, ex_ref.shape, 1)
        ex_ref[...] = (ri == d * heads + ci // hd).astype(BF16)

    xv = dt_ref[...] + bias_ref[...]
    dtv = jnp.maximum(xv, 0.0) + jnp.log1p(jnp.exp(-jnp.abs(xv)))
    da = dtv * (-jnp.exp(alog_ref[...]))
    qi = lax.broadcasted_iota(jnp.int32, (t, t), 0)
    si = lax.broadcasted_iota(jnp.int32, (t, t), 1)
    mask = (si - qi) * jnp.where(d == 0, 1, -1) <= 0
    acum = _dot_exact_rhs(mask.astype(BF16), da)

    acum_x = _dot_split_lhs(acum, ex_ref[...], 3)
    dtv_x = _dot_split_lhs(dtv, ex_ref[0:2 * ncol, :], 2)
    acum_t_all = jnp.transpose(acum)
    acum_t = jnp.where(d == 0, acum_t_all[0:heads], acum_t_all[heads:2 * heads])
    a_end = jnp.where(d == 0, acum_x[t - 1:t, :], acum_x[0:1, :])

    xdt = xs_ref[...].astype(F32) * dtv_x
    xdt_b = xdt.astype(BF16)
    scale_q = jnp.exp(acum_x)
    xw = (xdt * jnp.exp(a_end - acum_x)).astype(BF16)
    decay_end = jnp.exp(a_end)

    lane = lax.broadcasted_iota(jnp.int32, (t, LANES), 1)
    for g in range(SSD_GROUPS):
        gs = slice(g * width, (g + 1) * width)
        cg = c_ref[:, g * n:(g + 1) * n]
        bg = b_ref[:, g * n:(g + 1) * n]
        cb = lax.dot_general(cg, bg, (((1,), (1,)), ((), ())), preferred_element_type=F32)
        state = st_ref[g]
        y_state = jnp.dot(cg, state.astype(BF16), preferred_element_type=F32)
        bg_t = jnp.transpose(bg.astype(F32)).astype(BF16)
        st_ref[g] = state * decay_end[:, gs] + jnp.dot(bg_t, xw[:, gs], preferred_element_type=F32)
        for p in range(heads_per_group // 2):
            sl = slice(g * width + p * LANES, g * width + (p + 1) * LANES)
            mats = []
            for hh in (g * heads_per_group + 2 * p, g * heads_per_group + 2 * p + 1):
                a_q = acum_x[:, hh * hd:hh * hd + 1]
                a_s = acum_t[hh:hh + 1, :]
                mats.append(cb * jnp.exp(jnp.where(mask, a_q - a_s, -jnp.inf)))
            lcat = jnp.concatenate(mats, axis=1).astype(BF16)
            xp = xdt_b[:, sl]
            xcat = jnp.concatenate([jnp.where(lane < hd, xp, jnp.zeros_like(xp)),
                                    jnp.where(lane >= hd, xp, jnp.zeros_like(xp))], axis=0)
            y_in = jnp.dot(lcat, xcat, preferred_element_type=F32)
            y_ref[0, :, sl] = (y_in + y_state[:, p * LANES:(p + 1) * LANES] * scale_q[:, sl]).astype(y_ref.dtype)


def _ssd_scan(xbc, p2, dt_col_idx, dt_bias, a_log, batch, seq, ctx_len, d_inner):
    m = xbc.shape[0]
    t = SSD_CHUNK
    heads = d_inner // SSD_HEAD_DIM
    hpg = heads // SSD_GROUPS
    width = hpg * SSD_HEAD_DIM
    n = SSD_STATE
    nc_ctx, nc_lat = ctx_len // t, seq // t
    nch = nc_ctx + nc_lat
    ctx_base = batch * nc_lat
    ncol = _round_up(2 * heads, LANES)

    def row_block(b, d, i):
        ctx_blk = ctx_base + b * nc_ctx + jnp.where(d == 0, i, nc_ctx - 1 - i)
        lat_blk = b * nc_lat + jnp.where(d == 0, i - nc_ctx, nch - 1 - i)
        return jnp.where(i < nc_ctx, ctx_blk, lat_blk)

    gn = SSD_GROUPS * n
    b_col = d_inner // gn
    pad = ncol - 2 * heads
    bias = jnp.pad(dt_bias.reshape(1, 2 * heads).astype(F32), ((0, 0), (0, pad)))
    alog = jnp.pad(a_log.reshape(1, 2 * heads).astype(F32), ((0, 0), (0, pad)))
    return pl.pallas_call(
        functools.partial(_ssd_kernel, heads=heads, heads_per_group=hpg),
        grid=(batch, 2, nch),
        in_specs=[pl.BlockSpec((t, d_inner), lambda b, d, i: (row_block(b, d, i), 0)),
                  pl.BlockSpec((t, gn), lambda b, d, i: (row_block(b, d, i), b_col)),
                  pl.BlockSpec((t, gn), lambda b, d, i: (row_block(b, d, i), b_col + 1)),
                  pl.BlockSpec((t, ncol), lambda b, d, i: (row_block(b, d, i), dt_col_idx)),
                  pl.BlockSpec((1, ncol), lambda b, d, i: (0, 0)),
                  pl.BlockSpec((1, ncol), lambda b, d, i: (0, 0))],
        out_specs=pl.BlockSpec((1, t, d_inner), lambda b, d, i: (d, row_block(b, d, i), 0)),
        out_shape=jax.ShapeDtypeStruct((2, m, d_inner), BF16),
        scratch_shapes=[pltpu.VMEM((SSD_GROUPS, n, width), F32), pltpu.VMEM((3 * ncol, d_inner), BF16)],
        compiler_params=_cp("parallel", "parallel", "arbitrary"),
        name="ssd_scan",
    )(xbc, xbc, xbc, p2, bias, alog)


def _round_up(v, mult):
    return -(-v // mult) * mult


def _ssd_finish_kernel(y_ref, xs_ref, z_ref, d_ref, g_ref, o_ref):
    y = y_ref[0].astype(F32) + y_ref[1].astype(F32) + d_ref[...] * xs_ref[...].astype(F32)
    zv = z_ref[...].astype(F32)
    y = y * _silu(zv)
    gw = y.shape[1] // SSD_GROUPS
    for gi in range(SSD_GROUPS):
        sl = slice(gi * gw, (gi + 1) * gw)
        yg = y[:, sl]
        inv = lax.rsqrt(jnp.mean(yg * yg, axis=-1, keepdims=True) + NORM_EPS)
        o_ref[:, sl] = (yg * inv * g_ref[:, sl]).astype(o_ref.dtype)


def _ssd_finish(y2, xbc, p1, d_skip, ssd_norm, d_inner, tr):
    m = xbc.shape[0]
    d_x = jnp.repeat(d_skip.astype(F32), SSD_HEAD_DIM).reshape(1, d_inner)
    return pl.pallas_call(
        _ssd_finish_kernel,
        grid=(m // tr,),
        in_specs=[pl.BlockSpec((2, tr, d_inner), lambda i: (0, i, 0)),
                  pl.BlockSpec((tr, d_inner), lambda i: (i, 0)),
                  pl.BlockSpec((tr, d_inner), lambda i: (i, 0)),
                  pl.BlockSpec((1, d_inner), lambda i: (0, 0)),
                  pl.BlockSpec((1, d_inner), lambda i: (0, 0))],
        out_specs=pl.BlockSpec((tr, d_inner), lambda i: (i, 0)),
        out_shape=jax.ShapeDtypeStruct((m, d_inner), BF16),
        compiler_params=_cp("parallel"),
        name="ssd_finish",
    )(y2, xbc, p1, d_x, ssd_norm.reshape(1, d_inner).astype(F32))


def _mla_attn_kernel(*refs, ctx_len, with_latent, key_chunk):
    if with_latent:
        q_ref, kc_ref, vc_ref, pc_ref, kl_ref, vl_ref, pl_ref, cos_ref, sin_ref, o_ref, kt_ref, vs_ref = refs
        seq = kl_ref.shape[0]
    else:
        q_ref, kc_ref, vc_ref, pc_ref, _, o_ref, kt_ref, vs_ref = refs
        seq = 0

    @pl.when(pl.program_id(2) == 0)
    def _():
        lane = lax.broadcasted_iota(jnp.int32, (ctx_len, LANES), 1)
        kt_ref[0:LANES, 0:ctx_len] = jnp.transpose(kc_ref[...].astype(F32)).astype(BF16)
        kt_ref[LANES:, 0:ctx_len] = jnp.transpose(jnp.where(lane < MLA_ROPE, pc_ref[...], 0.0)).astype(BF16)
        vs_ref[0:ctx_len, :] = vc_ref[...]
        for a in range(0, seq, key_chunk):
            rows = slice(a, a + key_chunk)
            cols = slice(ctx_len + a, ctx_len + a + key_chunk)
            kt_ref[0:LANES, cols] = jnp.transpose(kl_ref[rows, :].astype(F32)).astype(BF16)
            blk = pl_ref[rows, :]
            rot = blk * cos_ref[rows, :] + pltpu.roll(blk, MLA_ROPE, 1) * sin_ref[rows, :]
            kt_ref[LANES:, cols] = jnp.transpose(rot).astype(BF16)
            vs_ref[cols, :] = vl_ref[rows, :]

    qv = q_ref[...]
    bounds = [(0, ctx_len)] + [(ctx_len + a, ctx_len + a + key_chunk) for a in range(0, seq, key_chunk)]
    m_run = acc = l_part = None
    for lo, hi in bounds:
        s = jnp.dot(qv, kt_ref[:, lo:hi], preferred_element_type=F32)
        m_new = jnp.max(s, axis=-1, keepdims=True)
        if m_run is not None:
            m_new = jnp.maximum(m_run, m_new)
        p = jnp.exp2(s - m_new)
        l_new = p[:, 0:LANES]
        for c in range(LANES, hi - lo, LANES):
            l_new = l_new + p[:, c:c + LANES]
        pv = jnp.dot(p.astype(BF16), vs_ref[lo:hi, :], preferred_element_type=F32)
        if m_run is None:
            acc, l_part = pv, l_new
        else:
            alpha = jnp.exp2(m_run - m_new)
            acc = alpha * acc + pv
            l_part = alpha * l_part + l_new
        m_run = m_new
    o_ref[...] = (acc / jnp.sum(l_part, axis=-1, keepdims=True)).astype(o_ref.dtype)


def _mla_attention(qh, kv, p2, kpe_idx, cos_t, sin_t, batch, seq, ctx_len, tq):
    m = qh.shape[0]
    heads = MLA_HEADS
    key_chunk = min(seq, MLA_KEY_CHUNK)
    cb = (batch * seq) // ctx_len
    nq = seq // tq
    out_shape = jax.ShapeDtypeStruct((m, heads * MLA_V), BF16)
    ctx_specs = [pl.BlockSpec((ctx_len, LANES), lambda b, h, i: (cb + b, 2 * h)),
                 pl.BlockSpec((ctx_len, LANES), lambda b, h, i: (cb + b, 2 * h + 1)),
                 pl.BlockSpec((ctx_len, LANES), lambda b, h, i: (cb + b, kpe_idx))]
    lat = pl.pallas_call(
        functools.partial(_mla_attn_kernel, ctx_len=ctx_len, with_latent=True, key_chunk=key_chunk),
        grid=(batch, heads, nq),
        in_specs=[pl.BlockSpec((tq, 2 * LANES), lambda b, h, i: (b * nq + i, h))] + ctx_specs + [
            pl.BlockSpec((seq, LANES), lambda b, h, i: (b, 2 * h)),
            pl.BlockSpec((seq, LANES), lambda b, h, i: (b, 2 * h + 1)),
            pl.BlockSpec((seq, LANES), lambda b, h, i: (b, kpe_idx)),
            pl.BlockSpec((seq, LANES), lambda b, h, i: (0, 0)),
            pl.BlockSpec((seq, LANES), lambda b, h, i: (0, 0))],
        out_specs=pl.BlockSpec((tq, MLA_V), lambda b, h, i: (b * nq + i, h)),
        out_shape=out_shape,
        scratch_shapes=[pltpu.VMEM((2 * LANES, ctx_len + seq), BF16),
                        pltpu.VMEM((ctx_len + seq, MLA_V), BF16)],
        compiler_params=_cp("parallel", "parallel", "arbitrary"),
        name="mla_attn_latent",
    )(qh, kv, kv, p2, kv, kv, p2, cos_t, sin_t)
    return pl.pallas_call(
        functools.partial(_mla_attn_kernel, ctx_len=ctx_len, with_latent=False, key_chunk=key_chunk),
        grid=(batch, heads, 1),
        in_specs=[pl.BlockSpec((ctx_len, 2 * LANES), lambda b, h, i: (cb + b, h))] + ctx_specs + [
            pl.BlockSpec(memory_space=pl.ANY)],
        out_specs=pl.BlockSpec((ctx_len, MLA_V), lambda b, h, i: (cb + b, h)),
        out_shape=out_shape,
        scratch_shapes=[pltpu.VMEM((2 * LANES, ctx_len), BF16), pltpu.VMEM((ctx_len, MLA_V), BF16)],
        input_output_aliases={4: 0},
        compiler_params=_cp("parallel", "parallel", "arbitrary"),
        name="mla_attn_ctx",
    )(qh, kv, kv, p2, lat)


def _na_kernel(q_ref, k_ref, v_ref, kc_ref, vc_ref, pair_ref, o_ref, tab_ref, *, grid_rows, n_blk,
               blocks_per_step, key_chunk, plan):
    step = pl.program_id(2)

    @pl.when(jnp.logical_and(pl.program_id(1) == 0, step == 0))
    def _():
        for case, per_row in enumerate(plan):
            for qa, per_pair in enumerate(per_row):
                for i, src in enumerate(per_pair):
                    if src < 0:
                        blk = jnp.full((GRID_W, 2 * GRID_W), MASK_NEG, F32)
                    else:
                        blk = pair_ref[0, src] * LOG2_E
                    tab_ref[case, qa * GRID_W:(qa + 1) * GRID_W, i * 2 * GRID_W:(i + 1) * 2 * GRID_W] = blk

    nt = (((1,), (1,)), ((), ()))
    tq = NA_Q_ROWS * GRID_W
    n_keys = NA_K_ROWS * GRID_W
    for sub in range(blocks_per_step):
        j = step * blocks_per_step + sub
        case = jnp.where(j == 0, 0, jnp.where(j == n_blk - 1, 2, 1))
        ws = jnp.clip(NA_Q_ROWS * j - NA_WIN_H // 2, 0, grid_rows - NA_K_ROWS)
        rows = pl.ds(sub * tq, tq)
        qv = q_ref[rows, :]
        m_run = acc = l_part = None
        for c0 in [None] + list(range(0, n_keys, key_chunk)):
            if c0 is None:
                s = lax.dot_general(qv, kc_ref[...], nt, preferred_element_type=F32)
                vv = vc_ref[...]
            else:
                krows = pl.ds(pl.multiple_of(ws * GRID_W + c0, GRID_W), key_chunk)
                s = (lax.dot_general(qv, k_ref[krows, :], nt, preferred_element_type=F32)
                     + tab_ref[case, :, c0:c0 + key_chunk])
                vv = v_ref[krows, :]
            m_new = jnp.max(s, axis=-1, keepdims=True)
            if m_run is not None:
                m_new = jnp.maximum(m_run, m_new)
            p = jnp.exp2(s - m_new)
            l_new = p[:, 0:LANES]
            for c in range(LANES, s.shape[1], LANES):
                l_new = l_new + p[:, c:c + LANES]
            pv = jnp.dot(p.astype(BF16), vv, preferred_element_type=F32)
            if m_run is None:
                acc, l_part = pv, l_new
            else:
                alpha = jnp.exp2(m_run - m_new)
                acc = alpha * acc + pv
                l_part = alpha * l_part + l_new
            m_run = m_new
        o_ref[rows, :] = (acc / jnp.sum(l_part, axis=-1, keepdims=True)).astype(o_ref.dtype)


def _attn_small_kernel(q_ref, k_ref, v_ref, _, o_ref):
    s = lax.dot_general(q_ref[...], k_ref[...], (((1,), (1,)), ((), ())), preferred_element_type=F32)
    p = jnp.exp2(s - jnp.max(s, axis=-1, keepdims=True))
    o = jnp.dot(p.astype(BF16), v_ref[...], preferred_element_type=F32)
    o_ref[...] = (o / jnp.sum(p, axis=-1, keepdims=True)).astype(o_ref.dtype)


def _na_pair_blocks(rpb):
    qc = np.arange(GRID_W)[:, None]
    kc = np.arange(GRID_W)[None, :]
    c0 = np.clip(qc - NA_WIN_W // 2, 0, GRID_W - NA_WIN_W)
    col_ok = (kc >= c0) & (kc < c0 + NA_WIN_W)
    dc = np.clip(kc - qc + NA_WIN_W - 1, 0, 2 * NA_WIN_W - 2)
    a = jnp.take(rpb.astype(F32), jnp.asarray(dc), axis=2)
    a = jnp.where(jnp.asarray(col_ok), a, MASK_NEG)
    neg = jnp.full_like(a, MASK_NEG)
    return jnp.concatenate([jnp.concatenate([a[:, :-1], a[:, 1:]], axis=-1),
                            jnp.concatenate([a, neg], axis=-1),
                            jnp.concatenate([neg, a], axis=-1)], axis=1)


def _na_table_plan(grid_rows):
    n_blk = grid_rows // NA_Q_ROWS
    n_dr = 2 * NA_WIN_H - 1

    def block_plan(j):
        ws = int(np.clip(NA_Q_ROWS * j - NA_WIN_H // 2, 0, grid_rows - NA_K_ROWS))
        rows = []
        for qa in range(NA_Q_ROWS):
            qr = NA_Q_ROWS * j + qa
            r0 = int(np.clip(qr - NA_WIN_H // 2, 0, grid_rows - NA_WIN_H))
            pairs = []
            for i in range(NA_K_ROWS // 2):
                kr0 = ws + 2 * i
                ok0 = r0 <= kr0 < r0 + NA_WIN_H
                ok1 = r0 <= kr0 + 1 < r0 + NA_WIN_H
                dr0 = kr0 - qr + NA_WIN_H - 1
                if ok0 and ok1:
                    pairs.append(dr0)
                elif ok0:
                    pairs.append(n_dr - 1 + dr0)
                elif ok1:
                    pairs.append(2 * n_dr - 1 + dr0 + 1)
                else:
                    pairs.append(-1)
            rows.append(tuple(pairs))
        return tuple(rows)

    plans = [block_plan(j) for j in range(n_blk)]
    cases = (plans[0], plans[min(1, n_blk - 1)], plans[n_blk - 1])
    for j in range(1, n_blk - 1):
        if plans[j] != cases[1]:
            raise ValueError("neighbourhood-attention middle blocks are not translation invariant")
    return cases


def _na_attention(qkv, rpb, batch, seq, ctx_len, d_model, ctx_out):
    m = qkv.shape[0]
    heads = NA_HEADS
    hd = d_model // heads
    grid_rows = seq // GRID_W
    n_blk = grid_rows // NA_Q_ROWS
    bps = NA_BLOCKS_PER_STEP if n_blk % NA_BLOCKS_PER_STEP == 0 else 1
    n_steps = n_blk // bps
    tq = bps * NA_Q_ROWS * GRID_W
    tk = NA_K_ROWS * GRID_W
    cb = (batch * seq) // ctx_len
    k0, v0 = d_model // hd, 2 * d_model // hd
    pairs = _na_pair_blocks(rpb)
    n_pairs = pairs.shape[1]
    out_shape = jax.ShapeDtypeStruct((m, d_model), BF16)

    lat = pl.pallas_call(
        functools.partial(_na_kernel, grid_rows=grid_rows, n_blk=n_blk, blocks_per_step=bps,
                          key_chunk=min(tk, NA_KEY_CHUNK), plan=_na_table_plan(grid_rows)),
        grid=(heads, batch, n_steps),
        in_specs=[pl.BlockSpec((tq, hd), lambda h, b, j: (b * n_steps + j, h)),
                  pl.BlockSpec((seq, hd), lambda h, b, j: (b, k0 + h)),
                  pl.BlockSpec((seq, hd), lambda h, b, j: (b, v0 + h)),
                  pl.BlockSpec((ctx_len, hd), lambda h, b, j: (cb + b, k0 + h)),
                  pl.BlockSpec((ctx_len, hd), lambda h, b, j: (cb + b, v0 + h)),
                  pl.BlockSpec((1, n_pairs, GRID_W, 2 * GRID_W), lambda h, b, j: (h, 0, 0, 0))],
        out_specs=pl.BlockSpec((tq, hd), lambda h, b, j: (b * n_steps + j, h)),
        out_shape=out_shape,
        scratch_shapes=[pltpu.VMEM((3, NA_Q_ROWS * GRID_W, tk), F32)],
        compiler_params=_cp("arbitrary", "arbitrary", "arbitrary"),
        name="na_attn_latent",
    )(qkv, qkv, qkv, qkv, qkv, pairs)
    if not ctx_out:
        return lat
    return pl.pallas_call(
        _attn_small_kernel,
        grid=(batch, heads),
        in_specs=[pl.BlockSpec((ctx_len, hd), lambda b, h: (cb + b, h)),
                  pl.BlockSpec((ctx_len, hd), lambda b, h: (cb + b, k0 + h)),
                  pl.BlockSpec((ctx_len, hd), lambda b, h: (cb + b, v0 + h)),
                  pl.BlockSpec(memory_space=pl.ANY)],
        out_specs=pl.BlockSpec((ctx_len, hd), lambda b, h: (cb + b, h)),
        out_shape=out_shape,
        input_output_aliases={3: 0},
        compiler_params=_cp("parallel", "parallel"),
        name="na_attn_ctx",
    )(qkv, qkv, qkv, lat)


def _rope_tables(seq, pad_rows):
    pos = np.arange(seq)
    n_freq = MLA_ROPE // 4
    inv_freq = jnp.power(ROPE_THETA, -jnp.arange(n_freq, dtype=F32) / n_freq)
    rows = jnp.asarray(pos // GRID_W, F32)[:, None] * inv_freq
    cols = jnp.asarray(pos % GRID_W, F32)[:, None] * inv_freq
    ang = jnp.concatenate([rows, rows, cols, cols], axis=-1)
    sign = np.tile(np.concatenate([-np.ones(n_freq), np.ones(n_freq)]), 2).astype(np.float32)
    cos_t = jnp.concatenate([jnp.cos(ang), jnp.zeros((seq, LANES - MLA_ROPE), F32)], axis=1)
    sin_t = jnp.concatenate([jnp.sin(ang) * sign, jnp.zeros((seq, LANES - MLA_ROPE), F32)], axis=1)
    ident = jnp.concatenate([jnp.ones((pad_rows, MLA_ROPE), F32), jnp.zeros((pad_rows, LANES - MLA_ROPE), F32)], 1)
    return (jnp.concatenate([cos_t, ident], axis=0),
            jnp.concatenate([sin_t, jnp.zeros((pad_rows, LANES), F32)], axis=0))


def _rope_partner_perm():
    n_freq = MLA_ROPE // 4
    dd = np.arange(MLA_ROPE)
    return np.where(dd % (2 * n_freq) < n_freq, dd + n_freq, dd - n_freq)


def kernel(x, c, ctx, c_ctx, ada_w1, ada_w2, ada_b, norm_mix, norm_ffn, ffn_w_gate, ffn_w_up, ffn_w_down,
           hyb_w_in, ssd_conv_w, ssd_conv_b, ssd_dt_bias, ssd_a_log, ssd_d, ssd_norm,
           mla_q_norm, mla_w_q_up, mla_kv_norm, mla_w_kv_up, hyb_w_out,
           na_w_qkv, na_rpb, na_w_out, final_norm):
    batch, seq, d_model = x.shape
    ctx_len = ctx.shape[1]
    depth = ada_w1.shape[0]
    n_lat = batch * seq
    m = n_lat + batch * ctx_len
    d_inner = ssd_norm.shape[-1]
    heads = d_inner // SSD_HEAD_DIM
    conv_dim = ssd_conv_w.shape[1]
    q_rank = mla_q_norm.shape[-1]
    kv_rank = mla_kv_norm.shape[-1]
    tseg = int(np.gcd(seq, batch * ctx_len))
    tseg = min(tseg, 1024)
    tm_big = tseg
    tm_wide = _row_tile(m, WIDE_ROW_TILE)
    tr = min(tseg, 256)

    def seg(row0):
        return jnp.where(row0 < n_lat, row0 // seq, batch)

    h = jnp.concatenate([x.reshape(n_lat, d_model), ctx.reshape(batch * ctx_len, d_model)], axis=0)
    cond = jnp.zeros((COND_ROWS, d_model), F32).at[:batch].set(c).at[batch].set(c_ctx)
    mods_all = _ada(cond, ada_w1, ada_w2, ada_b).reshape(depth, COND_ROWS * N_MOD, 1, d_model)

    dt_cols = _round_up(2 * heads, LANES)
    perm = _rope_partner_perm()
    cos_t, sin_t = _rope_tables(seq, tm_big)
    lat_tiles = n_lat // tm_big
    tiles_per_seq = seq // tm_big

    def rope_table_idx(i):
        return jnp.where(i < lat_tiles, i % tiles_per_seq, tiles_per_seq)

    ffn_w_down_b = ffn_w_down.astype(BF16)
    hyb_w_out_b = hyb_w_out.astype(BF16)
    na_w_out_b = na_w_out.astype(BF16)
    for layer in range(depth):
        ctx_out = layer < depth - 1
        rows = m if ctx_out else n_lat
        mods = mods_all[layer]
        u = _norm_rows(h, d_model, 0, norm_mix[layer], BF16, m, tr, mods, (0, 1), seg)
        if layer % 2 == 0:
            e = layer // 2
            w_in = hyb_w_in[e]
            s0 = d_inner + conv_dim
            s1 = s0 + 2 * heads
            s2 = s1 + q_rank
            s3 = s2 + kv_rank
            w_kpe = w_in[:, s3:]
            w_p2 = jnp.concatenate([w_in[:, s1:s2], w_in[:, s2:s3],
                                    jnp.pad(w_in[:, s0:s1], ((0, 0), (0, dt_cols - 2 * heads))),
                                    w_kpe, w_kpe[:, perm]], axis=1)
            kv_idx = q_rank // kv_rank
            dt_idx = (q_rank + kv_rank) // dt_cols
            kpe_idx = (q_rank + kv_rank + dt_cols) // LANES
            p1 = _mm(u, w_in[:, :s0].astype(BF16), BF16, tm_wide, _tile(s0, 512),
                     single_buffer_x=True)
            p2 = _mm(u, w_p2, F32, tm_big, _tile(w_p2.shape[1], 256))
            xbc = _conv_silu(p1, d_inner, conv_dim, ssd_conv_w[e], ssd_conv_b[e], n_lat, seq, ctx_len,
                             min(ctx_len, 256), _tile(int(np.gcd(conv_dim, d_inner)), 2048))
            y2 = _ssd_scan(xbc, p2, dt_idx, ssd_dt_bias[e], ssd_a_log[e], batch, seq, ctx_len, d_inner)
            ssd_out = _ssd_finish(y2, xbc, p1, ssd_d[e], ssd_norm[e], d_inner, tr)

            qn = _norm_rows(p2, q_rank, 0, mla_q_norm[e], BF16, m, tr)
            kvn = _norm_rows(p2, kv_rank, kv_idx, mla_kv_norm[e], BF16, m, tr)
            wq = mla_w_q_up[e].reshape(q_rank, MLA_HEADS, MLA_NOPE + MLA_ROPE)
            wq_rope = wq[:, :, MLA_NOPE:]
            wq3 = jnp.concatenate([wq[:, :, :MLA_NOPE], wq_rope, wq_rope[:, :, perm]], axis=2)
            wq3 = wq3.reshape(q_rank, MLA_HEADS * 2 * LANES).astype(BF16)
            qh = _mm_qrope(qn, wq3, cos_t, sin_t, tm_big, min(4, MLA_HEADS), rope_table_idx)
            kv = _mm(kvn, mla_w_kv_up[e].astype(BF16), BF16, tm_big, _tile(MLA_HEADS * 2 * LANES, 2048))
            attn = _mla_attention(qh, kv, p2, kpe_idx, cos_t, sin_t, batch, seq, ctx_len, min(seq, MLA_Q_TILE))
            h = _mm_res([ssd_out, attn], hyb_w_out_b, e, h, mods, 2, seg, rows, min(tseg, 512),
                        _tile(d_model, 512))
        else:
            o = layer // 2
            qkv = _mm(u, na_w_qkv, BF16, tm_wide, _tile(d_model, 512), layer=o, single_buffer_x=True,
                      scaled_cols=d_model, out_scale=float((d_model // NA_HEADS) ** -0.5 * LOG2_E))
            att = _na_attention(qkv, na_rpb[o], batch, seq, ctx_len, d_model, ctx_out)
            h = _mm_res([att], na_w_out_b, o, h, mods, 2, seg, rows, tm_big, _tile(d_model, 512))
        u2 = _norm_rows(h, d_model, 0, norm_ffn[layer], BF16, rows, tr, mods, (3, 4), seg)
        hidden = _mm_swiglu(u2, ffn_w_gate, ffn_w_up, layer, rows, _row_tile(rows, WIDE_ROW_TILE),
                            _tile(ffn_w_gate.shape[-1], 256))
        h = _mm_res([hidden], ffn_w_down_b, layer, h, mods, 5, seg, rows, tm_big,
                    _tile(d_model, 256), vmem_limit=62 * 2 ** 20)
    out = _norm_rows(h, d_model, 0, final_norm, F32, n_lat, tr)
    return out.reshape(batch, seq, d_model)
```
